```python
import math
import jax, jax.numpy as jnp
from jax import lax
import numpy as np

D_MODEL = 1024
BATCH = 16
SEQ = 4096
DEPTH = 2
DEC_BATCH = 1
DEC_SEQ = 16384
PAST_LEN = 128

GRID_W = 64
HEAD_DIM = 64
EPS = 1e-6
Q_BLOCK = 128

NA_HEADS = 8
NA_WIN_R = 8
NA_WIN_C = 16
NA_COL_BLK = 16
NA_KEY_COLS = NA_COL_BLK + NA_WIN_C
NA_W = NA_HEADS * HEAD_DIM

DIFF_HEADS = 4
DIFF_QK_DIM = 64
DIFF_V_DIM = 2 * DIFF_QK_DIM
DIFF_QK_W = DIFF_HEADS * 2 * DIFF_QK_DIM
DIFF_V_W = DIFF_HEADS * DIFF_V_DIM

IN0_W = 3 * NA_W + 2 * DIFF_QK_W + DIFF_V_W
MIX0_W = NA_W + DIFF_V_W

MLA_HEADS = 16
MLA_Q_RANK = 384
MLA_KV_RANK = 256
MLA_NOPE = 64
MLA_ROPE = 32
MLA_V = 64
ROPE_THETA = 10000.0
MLA_DOWN_W = MLA_Q_RANK + MLA_KV_RANK + MLA_ROPE

D_FF = -(-8 * D_MODEL // (3 * 256)) * 256

kernel_name = "hybrid_na_diff_mla_encoder"


def rmsnorm(x, g):
    xf = x.astype(jnp.float32)
    y = xf * lax.rsqrt(jnp.mean(xf * xf, axis=-1, keepdims=True) + EPS)
    return (y * g.astype(jnp.float32)).astype(x.dtype)


def alibi_slopes(n):
    return (2.0 ** (-8.0 * np.arange(1, n + 1) / n)).astype(np.float32)


def apply_rope(x, pos):
    r = x.shape[-1]
    half = r // 2
    inv = jnp.asarray(ROPE_THETA ** (-np.arange(half, dtype=np.float32) * 2.0 / r), dtype=jnp.float32)
    ang = pos.astype(jnp.float32)[:, None] * inv[None, :]
    cos = jnp.cos(ang)[None, :, None, :]
    sin = jnp.sin(ang)[None, :, None, :]
    xf = x.astype(jnp.float32)
    x1, x2 = xf[..., :half], xf[..., half:]
    return jnp.concatenate([x1 * cos - x2 * sin, x2 * cos + x1 * sin], axis=-1).astype(x.dtype)


def neighbourhood_attention(q, k, v, rpb):
    B, S, H, d = q.shape
    rows = S // GRID_W
    kr = min(NA_WIN_R, rows)
    n_cb = GRID_W // NA_COL_BLK
    qg = q.reshape(B, rows, n_cb, NA_COL_BLK, H, d)
    kg = k.reshape(B, rows, GRID_W, H, d)
    vg = v.reshape(B, rows, GRID_W, H, d)
    cb_start = np.arange(n_cb) * NA_COL_BLK
    kc0 = np.clip(cb_start - NA_WIN_C // 2, 0, GRID_W - NA_KEY_COLS)
    key_cols = kc0[:, None] + np.arange(NA_KEY_COLS)
    q_cols = cb_start[:, None] + np.arange(NA_COL_BLK)
    win_c0 = np.clip(q_cols - NA_WIN_C // 2, 0, GRID_W - NA_WIN_C)
    kc = key_cols[:, None, :]
    col_ok = (kc >= win_c0[..., None]) & (kc < win_c0[..., None] + NA_WIN_C)
    col_idx = np.clip(kc - q_cols[..., None] + NA_WIN_C - 1, 0, 2 * NA_WIN_C - 2)
    rpb_c = rpb[:, :, col_idx]
    mask = jnp.asarray(col_ok)[:, :, None, :]
    scale = d ** -0.5

    def one_row(r):
        rs = jnp.clip(r - kr // 2, 0, rows - kr)
        k_blk = lax.dynamic_slice_in_dim(kg, rs, kr, axis=1)[:, :, key_cols]
        v_blk = lax.dynamic_slice_in_dim(vg, rs, kr, axis=1)[:, :, key_cols]
        q_row = lax.dynamic_index_in_dim(qg, r, axis=1, keepdims=False)
        s = jnp.einsum('bnqhd,brnkhd->bhnqrk', q_row, k_blk,
                       preferred_element_type=jnp.float32) * scale
        row_idx = rs + jnp.arange(kr) - r + NA_WIN_R - 1
        bias = jnp.take(rpb_c, row_idx, axis=1).transpose(0, 2, 3, 1, 4)
        s = jnp.where(mask, s + bias.astype(jnp.float32), -jnp.inf)
        p = jax.nn.softmax(s, axis=(-2, -1)).astype(v.dtype)
        o = jnp.einsum('bhnqrk,brnkhd->bnqhd', p, v_blk)
        return o.reshape(B, GRID_W, H, d)

    out = lax.map(one_row, jnp.arange(rows))
    return out.transpose(1, 0, 2, 3, 4).reshape(B, S, H, d)


def diff_attention(q, k, v, lam, sub_g, lam_init):
    B, S, H2, dq = q.shape
    H = H2 // 2
    dv = v.shape[-1]
    nblk = S // Q_BLOCK
    slopes = jnp.asarray(np.repeat(alibi_slopes(H), 2))
    kh = k.transpose(0, 2, 1, 3)
    vh = v.transpose(0, 2, 1, 3)
    qb = q.reshape(B, nblk, Q_BLOCK, H2, dq).transpose(1, 0, 3, 2, 4)
    kpos = jnp.arange(S)
    scale = dq ** -0.5

    def one_block(args):
        qblk, i = args
        qpos = i * Q_BLOCK + jnp.arange(Q_BLOCK)
        dist = jnp.abs(qpos[:, None] - kpos[None, :]).astype(jnp.float32)
        s = (jnp.einsum('bhqd,bhkd->bhqk', qblk, kh, preferred_element_type=jnp.float32) * scale
             - slopes[:, None, None] * dist)
        p = jax.nn.softmax(s, axis=-1).reshape(B, H, 2, Q_BLOCK, S)
        a = (p[:, :, 0] - lam * p[:, :, 1]).astype(v.dtype)
        return jnp.einsum('bhqk,bhkd->bqhd', a, vh)

    o = lax.map(one_block, (qb, jnp.arange(nblk)))
    o = o.transpose(1, 0, 2, 3, 4).reshape(B, S, H, dv)
    return rmsnorm(o, sub_g) * (1.0 - lam_init)


def mla_attention(h, w_down, g_q, g_kv, w_uq, w_ukv):
    B, S, _ = h.shape
    H = MLA_HEADS
    c = h @ w_down
    cq = c[..., :MLA_Q_RANK]
    ckv = c[..., MLA_Q_RANK:MLA_Q_RANK + MLA_KV_RANK]
    k_rope_in = c[..., MLA_Q_RANK + MLA_KV_RANK:]
    q = (rmsnorm(cq, g_q) @ w_uq).reshape(B, S, H, MLA_NOPE + MLA_ROPE)
    kv = (rmsnorm(ckv, g_kv) @ w_ukv).reshape(B, S, H, MLA_NOPE + MLA_V)
    pos = jnp.arange(S)
    q_nope = q[..., :MLA_NOPE]
    q_rope = apply_rope(q[..., MLA_NOPE:], pos)
    k_rope = apply_rope(k_rope_in[:, :, None, :], pos)[:, :, 0]
    kn = kv[..., :MLA_NOPE].transpose(0, 2, 1, 3)
    vh = kv[..., MLA_NOPE:].transpose(0, 2, 1, 3)
    nblk = S // Q_BLOCK
    qn_b = q_nope.reshape(B, nblk, Q_BLOCK, H, MLA_NOPE).transpose(1, 0, 3, 2, 4)
    qr_b = q_rope.reshape(B, nblk, Q_BLOCK, H, MLA_ROPE).transpose(1, 0, 3, 2, 4)
    scale = (MLA_NOPE + MLA_ROPE) ** -0.5

    def one_block(args):
        qn, qr = args
        s = (jnp.einsum('bhqd,bhkd->bhqk', qn, kn, preferred_element_type=jnp.float32)
             + jnp.einsum('bhqr,bkr->bhqk', qr, k_rope, preferred_element_type=jnp.float32)) * scale
        p = jax.nn.softmax(s, axis=-1).astype(h.dtype)
        return jnp.einsum('bhqk,bhkd->bqhd', p, vh)

    o = lax.map(one_block, (qn_b, qr_b))
    return o.transpose(1, 0, 2, 3, 4).reshape(B, S, H * MLA_V)


def swiglu(h, w_in, w_out):
    gu = h @ w_in
    g, u = gu[..., :D_FF], gu[..., D_FF:]
    return (jax.nn.silu(g) * u) @ w_out


def mixer_na_diff(h, w_in, rpb, lam_q1, lam_k1, lam_q2, lam_k2, g_sub, w_out, layer_idx):
    B, S, _ = h.shape
    proj = h @ w_in
    o0 = 3 * NA_W
    na_q = proj[..., 0:NA_W].reshape(B, S, NA_HEADS, HEAD_DIM)
    na_k = proj[..., NA_W:2 * NA_W].reshape(B, S, NA_HEADS, HEAD_DIM)
    na_v = proj[..., 2 * NA_W:o0].reshape(B, S, NA_HEADS, HEAD_DIM)
    d_q = proj[..., o0:o0 + DIFF_QK_W].reshape(B, S, 2 * DIFF_HEADS, DIFF_QK_DIM)
    d_k = proj[..., o0 + DIFF_QK_W:o0 + 2 * DIFF_QK_W].reshape(B, S, 2 * DIFF_HEADS, DIFF_QK_DIM)
    d_v = proj[..., o0 + 2 * DIFF_QK_W:].reshape(B, S, DIFF_HEADS, DIFF_V_DIM)
    a = neighbourhood_attention(na_q, na_k, na_v, rpb)
    lam_init = 0.8 - 0.6 * math.exp(-0.3 * layer_idx)
    f32 = jnp.float32
    lam = (jnp.exp(jnp.sum(lam_q1.astype(f32) * lam_k1.astype(f32)))
           - jnp.exp(jnp.sum(lam_q2.astype(f32) * lam_k2.astype(f32))) + lam_init)
    b = diff_attention(d_q, d_k, d_v, lam, g_sub, lam_init)
    o = jnp.concatenate([a.reshape(B, S, NA_W), b.reshape(B, S, DIFF_V_W)], axis=-1)
    return o @ w_out


def trunk(x, p0, p1):
    layers = [p0, p1]
    for l in range(DEPTH):
        p = layers[l]
        h = rmsnorm(x, p['g_pre_mix'])
        if l % 2 == 0:
            m = mixer_na_diff(h, p['w_in'], p['rpb'], p['lam_q1'], p['lam_k1'], p['lam_q2'],
                              p['lam_k2'], p['g_sub'], p['w_out'], l)
        else:
            m = mla_attention(h, p['w_down'], p['g_q'], p['g_kv'], p['w_uq'], p['w_ukv']) @ p['w_out']
        x = x + rmsnorm(m, p['g_post_mix'])
        f = swiglu(rmsnorm(x, p['g_pre_ffn']), p['w_ffn_in'], p['w_ffn_out'])
        x = x + rmsnorm(f, p['g_post_ffn'])
    return x


def setup_inputs(seed: int = 0) -> dict:
    key = jax.random.key(seed)
    ks = iter(jax.random.split(key, 40))
    f32 = jnp.float32

    def w(shape):
        return jax.random.normal(next(ks), shape, f32) * (shape[0] ** -0.5)

    def gain(n):
        return 1.0 + 0.05 * jax.random.normal(next(ks), (n,), f32)

    d = {}
    d['x_prompt'] = jax.random.normal(next(ks), (BATCH, SEQ, D_MODEL), f32)
    d['x_sample'] = jax.random.normal(next(ks), (DEC_BATCH, DEC_SEQ, D_MODEL), f32)
    d['l0_g_pre_mix'] = gain(D_MODEL)
    d['l0_w_in'] = w((D_MODEL, IN0_W))
    d['l0_rpb'] = 0.1 * jax.random.normal(next(ks), (NA_HEADS, 2 * NA_WIN_R - 1, 2 * NA_WIN_C - 1), f32)
    d['l0_lam_q1'] = 0.1 * jax.random.normal(next(ks), (DIFF_QK_DIM,), f32)
    d['l0_lam_k1'] = 0.1 * jax.random.normal(next(ks), (DIFF_QK_DIM,), f32)
    d['l0_lam_q2'] = 0.1 * jax.random.normal(next(ks), (DIFF_QK_DIM,), f32)
    d['l0_lam_k2'] = 0.1 * jax.random.normal(next(ks), (DIFF_QK_DIM,), f32)
    d['l0_g_sub'] = gain(DIFF_V_DIM)
    d['l0_w_out'] = w((MIX0_W, D_MODEL))
    d['l0_g_post_mix'] = gain(D_MODEL)
    d['l0_g_pre_ffn'] = gain(D_MODEL)
    d['l0_w_ffn_in'] = w((D_MODEL, 2 * D_FF))
    d['l0_w_ffn_out'] = w((D_FF, D_MODEL))
    d['l0_g_post_ffn'] = gain(D_MODEL)
    d['l1_g_pre_mix'] = gain(D_MODEL)
    d['l1_w_down'] = w((D_MODEL, MLA_DOWN_W))
    d['l1_g_q'] = gain(MLA_Q_RANK)
    d['l1_g_kv'] = gain(MLA_KV_RANK)
    d['l1_w_uq'] = w((MLA_Q_RANK, MLA_HEADS * (MLA_NOPE + MLA_ROPE)))
    d['l1_w_ukv'] = w((MLA_KV_RANK, MLA_HEADS * (MLA_NOPE + MLA_V)))
    d['l1_w_out'] = w((MLA_HEADS * MLA_V, D_MODEL))
    d['l1_g_post_mix'] = gain(D_MODEL)
    d['l1_g_pre_ffn'] = gain(D_MODEL)
    d['l1_w_ffn_in'] = w((D_MODEL, 2 * D_FF))
    d['l1_w_ffn_out'] = w((D_FF, D_MODEL))
    d['l1_g_post_ffn'] = gain(D_MODEL)
    return d


def reference(x_prompt, x_sample,
              l0_g_pre_mix, l0_w_in, l0_rpb, l0_lam_q1, l0_lam_k1, l0_lam_q2, l0_lam_k2, l0_g_sub,
              l0_w_out, l0_g_post_mix, l0_g_pre_ffn, l0_w_ffn_in, l0_w_ffn_out, l0_g_post_ffn,
              l1_g_pre_mix, l1_w_down, l1_g_q, l1_g_kv, l1_w_uq, l1_w_ukv, l1_w_out, l1_g_post_mix,
              l1_g_pre_ffn, l1_w_ffn_in, l1_w_ffn_out, l1_g_post_ffn):
    p0 = dict(g_pre_mix=l0_g_pre_mix, w_in=l0_w_in, rpb=l0_rpb, lam_q1=l0_lam_q1, lam_k1=l0_lam_k1,
              lam_q2=l0_lam_q2, lam_k2=l0_lam_k2, g_sub=l0_g_sub, w_out=l0_w_out,
              g_post_mix=l0_g_post_mix, g_pre_ffn=l0_g_pre_ffn, w_ffn_in=l0_w_ffn_in,
              w_ffn_out=l0_w_ffn_out, g_post_ffn=l0_g_post_ffn)
    p1 = dict(g_pre_mix=l1_g_pre_mix, w_down=l1_w_down, g_q=l1_g_q, g_kv=l1_g_kv, w_uq=l1_w_uq,
              w_ukv=l1_w_ukv, w_out=l1_w_out, g_post_mix=l1_g_post_mix, g_pre_ffn=l1_g_pre_ffn,
              w_ffn_in=l1_w_ffn_in, w_ffn_out=l1_w_ffn_out, g_post_ffn=l1_g_post_ffn)
    y_prompt = trunk(x_prompt, p0, p1)
    y_sample = trunk(x_sample, p0, p1)
    return (y_prompt, y_sample)
```

```python
import functools
import math

import numpy as np
import jax
import jax.numpy as jnp
from jax import lax
from jax.experimental import pallas as pl
from jax.experimental.pallas import tpu as pltpu

F32 = jnp.float32
BF16 = jnp.bfloat16

D_MODEL = 1024
GRID_W = 64
HEAD_DIM = 64
EPS = 1e-6

NA_HEADS = 8
NA_WIN_R = 8
NA_WIN_C = 16
NA_W = NA_HEADS * HEAD_DIM
NA_ROWS_PER_STEP = 8
NA_KEY_ROWS = 16
NA_KEY_BLOCK_ROWS = 4

DIFF_HEADS = 4
DIFF_QK_DIM = 64
DIFF_V_DIM = 128
DIFF_W = 512

MLA_HEADS = 16
MLA_Q_RANK = 384
MLA_KV_RANK = 256
MLA_NOPE = 64
MLA_ROPE = 32
MLA_V = 64
MLA_QK = MLA_NOPE + MLA_ROPE
MLA_QK_PAD = 128
ROPE_THETA = 10000.0

D_FF = 2816
FF_CHUNK = 256

VMEM_LIMIT = 56 * 1024 * 1024

TOKEN_TILE = 512
ATT_TQ = 256
ATT_TK = 1024

_NT = (((1,), (1,)), ((), ()))


def _rms(x, g):
    ms = jnp.mean(x * x, axis=-1, keepdims=True)
    return (x * lax.rsqrt(ms + EPS)) * g


def _const_spec(shape):
    nd = len(shape)
    return pl.BlockSpec(shape, lambda *_: (0,) * nd, pipeline_mode=pl.Buffered(1))


def _params(n_axes):
    return pltpu.CompilerParams(
        dimension_semantics=("parallel",) * n_axes, vmem_limit_bytes=VMEM_LIMIT)


def _proj0_kernel(x_ref, g_ref, w_ref, wkt_ref,
                  naq_ref, nak_ref, nav_ref, dq_ref, dv_ref, dkt_ref):
    h = _rms(x_ref[...], g_ref[...]).astype(BF16)
    for i, o_ref in enumerate((naq_ref, nak_ref, nav_ref, dq_ref, dv_ref)):
        o_ref[...] = jnp.dot(h, w_ref[:, i * 512:(i + 1) * 512],
                             preferred_element_type=F32).astype(BF16)
    dkt_ref[...] = lax.dot_general(wkt_ref[...], h, _NT,
                                   preferred_element_type=F32).astype(BF16)


def _proj0(x, g, w, wkt, B, S):
    N = B * S
    tn = TOKEN_TILE
    spb = S // tn
    tok = pl.BlockSpec((tn, 512), lambda i: (i, 0))
    return pl.pallas_call(
        _proj0_kernel,
        grid=(N // tn,),
        in_specs=[pl.BlockSpec((tn, D_MODEL), lambda i: (i, 0)),
                  _const_spec((1, D_MODEL)),
                  _const_spec((D_MODEL, 5 * 512)),
                  _const_spec((512, D_MODEL))],
        out_specs=[tok, tok, tok, tok, tok,
                   pl.BlockSpec((None, 512, tn), lambda i: (i // spb, 0, i % spb))],
        out_shape=[jax.ShapeDtypeStruct((N, 512), BF16)] * 5
        + [jax.ShapeDtypeStruct((B, 512, S), BF16)],
        compiler_params=_params(1),
        name="proj0",
    )(x, g, w, wkt)


def _na_kernel(q_ref, k0, k1, k2, k3, v0, v1, v2, v3, bias_ref, o_ref,
               kbuf, vbuf, *, rows):
    rb = pl.program_id(1)
    blk = NA_KEY_BLOCK_ROWS * GRID_W
    for i, (kr, vr) in enumerate(((k0, v0), (k1, v1), (k2, v2), (k3, v3))):
        kbuf[i * blk:(i + 1) * blk, :] = kr[...]
        vbuf[i * blk:(i + 1) * blk, :] = vr[...]
    kw0 = jnp.clip(rb * NA_ROWS_PER_STEP - NA_WIN_R // 2, 0, rows - NA_KEY_ROWS)
    nkeys = NA_WIN_R * GRID_W

    def row_body(rl, carry):
        r = rb * NA_ROWS_PER_STEP + rl
        rs = jnp.clip(r - NA_WIN_R // 2, 0, rows - NA_WIN_R)
        off = pl.multiple_of((rs - kw0) * GRID_W, GRID_W)
        dr0 = rs - r + NA_WIN_R - 1
        qoff = pl.multiple_of(rl * GRID_W, GRID_W)
        qrow = q_ref[pl.ds(qoff, GRID_W), :]
        kwin = kbuf[pl.ds(off, nkeys), :]
        vwin = vbuf[pl.ds(off, nkeys), :]
        outs = []
        for h in range(NA_HEADS):
            sl = slice(h * HEAD_DIM, (h + 1) * HEAD_DIM)
            s = lax.dot_general(qrow[:, sl], kwin[:, sl], _NT,
                                preferred_element_type=F32)
            s = s + bias_ref[dr0, h]
            m = jnp.max(s, axis=-1, keepdims=True)
            p = jnp.exp(s - m)
            l = jnp.sum(p, axis=-1, keepdims=True)
            o = jnp.dot(p.astype(BF16), vwin[:, sl], preferred_element_type=F32)
            outs.append(o * (1.0 / l))
        o_ref[pl.ds(qoff, GRID_W), :] = jnp.concatenate(outs, axis=1).astype(BF16)
        return carry

    lax.fori_loop(0, NA_ROWS_PER_STEP, row_body, 0)


def _na_attn(q, k, v, bias, B, S):
    rows = S // GRID_W
    assert rows % NA_ROWS_PER_STEP == 0 and rows >= NA_KEY_ROWS
    N = B * S
    tq = NA_ROWS_PER_STEP * GRID_W
    blk = NA_KEY_BLOCK_ROWS * GRID_W
    bpb = rows // NA_KEY_BLOCK_ROWS

    def kv_spec(i):
        def idx(b, rb):
            kw0 = jnp.clip(rb * NA_ROWS_PER_STEP - NA_WIN_R // 2, 0, rows - NA_KEY_ROWS)
            return (b * bpb + kw0 // NA_KEY_BLOCK_ROWS + i, 0)
        return pl.BlockSpec((blk, NA_W), idx)

    kv_specs = [kv_spec(i) for i in range(NA_KEY_ROWS // NA_KEY_BLOCK_ROWS)]
    qspec = pl.BlockSpec((tq, NA_W), lambda b, rb: (b * (rows // NA_ROWS_PER_STEP) + rb, 0))
    return pl.pallas_call(
        functools.partial(_na_kernel, rows=rows),
        grid=(B, rows // NA_ROWS_PER_STEP),
        in_specs=[qspec] + kv_specs + kv_specs + [_const_spec(bias.shape)],
        out_specs=qspec,
        out_shape=jax.ShapeDtypeStruct((N, NA_W), BF16),
        scratch_shapes=[pltpu.VMEM((NA_KEY_ROWS * GRID_W, NA_W), BF16),
                        pltpu.VMEM((NA_KEY_ROWS * GRID_W, NA_W), BF16)],
        compiler_params=_params(2),
        name="na_attn",
    )(q, k, k, k, k, v, v, v, v, bias)


def _na_bias_table(rpb):
    c = np.arange(GRID_W)[:, None]
    cp = np.arange(GRID_W)[None, :]
    win0 = np.clip(c - NA_WIN_C // 2, 0, GRID_W - NA_WIN_C)
    ok = (cp >= win0) & (cp < win0 + NA_WIN_C)
    cidx = np.clip(cp - c + NA_WIN_C - 1, 0, 2 * NA_WIN_C - 2)
    t = rpb.astype(F32)[:, :, cidx]
    t = jnp.where(jnp.asarray(ok)[None, None], t, -jnp.inf)
    per = [t[:, d:d + NA_WIN_R].transpose(0, 2, 1, 3).reshape(NA_HEADS, GRID_W, NA_WIN_R * GRID_W)
           for d in range(NA_WIN_R)]
    return jnp.stack(per, axis=0)


def _diff_kernel(slopes_ref, q_ref, kt_ref, v_ref, lamv_ref, gsub_ref, o_ref,
                 m_ref, l_ref, acc_ref, *, S, tq, tk, lam_init):
    h = pl.program_id(1)
    qi = pl.program_id(2)
    slope = slopes_ref[h]
    q = q_ref[...]
    qs = (q[:, :DIFF_QK_DIM], q[:, DIFF_QK_DIM:])
    row = lax.broadcasted_iota(jnp.int32, (tq, tk), 0)
    col = lax.broadcasted_iota(jnp.int32, (tq, tk), 1)
    base = (row - col + qi * tq).astype(F32)

    m_ref[...] = jnp.full(m_ref.shape, -jnp.inf, F32)
    l_ref[...] = jnp.zeros(l_ref.shape, F32)
    acc_ref[...] = jnp.zeros(acc_ref.shape, F32)

    def body(kc, carry):
        k0 = pl.multiple_of(kc * tk, tk)
        kt = kt_ref[:, pl.ds(k0, tk)]
        v = v_ref[pl.ds(k0, tk), :]
        bias = slope * jnp.abs(base - k0.astype(F32))
        for j in range(2):
            s = jnp.dot(qs[j], kt[j * DIFF_QK_DIM:(j + 1) * DIFF_QK_DIM, :],
                        preferred_element_type=F32) - bias
            m_prev = m_ref[j]
            m_new = jnp.maximum(m_prev, jnp.max(s, axis=-1, keepdims=True))
            alpha = jnp.exp(m_prev - m_new)
            p = jnp.exp(s - m_new)
            l_ref[j] = alpha * l_ref[j] + jnp.sum(p, axis=-1, keepdims=True)
            acc_ref[j] = alpha * acc_ref[j] + jnp.dot(p.astype(BF16), v,
                                                      preferred_element_type=F32)
            m_ref[j] = m_new
        return carry

    lax.fori_loop(0, S // tk, body, 0)

    lv = lamv_ref[...]
    lam = (jnp.exp(jnp.sum(lv[0:1] * lv[1:2], axis=-1, keepdims=True))
           - jnp.exp(jnp.sum(lv[2:3] * lv[3:4], axis=-1, keepdims=True)) + lam_init)
    o = acc_ref[0] * (1.0 / l_ref[0]) - lam * (acc_ref[1] * (1.0 / l_ref[1]))
    o_ref[...] = (_rms(o, gsub_ref[...]) * (1.0 - lam_init)).astype(BF16)


def _diff_attn(q, kt, v, lamv, gsub, B, S, lam_init):
    N = B * S
    tq, tk = ATT_TQ, min(ATT_TK, S)
    nq = S // tq
    slopes = jnp.asarray(2.0 ** (-8.0 * np.arange(1, DIFF_HEADS + 1) / DIFF_HEADS), F32)
    return pl.pallas_call(
        functools.partial(_diff_kernel, S=S, tq=tq, tk=tk, lam_init=lam_init),
        grid=(B, DIFF_HEADS, nq),
        in_specs=[pl.BlockSpec(memory_space=pltpu.SMEM),
                  pl.BlockSpec((tq, 128), lambda b, h, i: (b * nq + i, h)),
                  pl.BlockSpec((None, 128, S), lambda b, h, i: (b, h, 0)),
                  pl.BlockSpec((S, 128), lambda b, h, i: (b, h)),
                  _const_spec((4, DIFF_QK_DIM)),
                  _const_spec((1, DIFF_V_DIM))],
        out_specs=pl.BlockSpec((tq, 128), lambda b, h, i: (b * nq + i, h)),
        out_shape=jax.ShapeDtypeStruct((N, DIFF_W), BF16),
        scratch_shapes=[pltpu.VMEM((2, tq, 1), F32), pltpu.VMEM((2, tq, 1), F32),
                        pltpu.VMEM((2, tq, DIFF_V_DIM), F32)],
        compiler_params=_params(3),
        name="diff_attn",
    )(slopes, q, kt, v, lamv, gsub)


def _ffn_kernel(*refs, n_a):
    a_refs = refs[:n_a]
    (x_ref, wo_ref, gpm_ref, gpf_ref, wg_ref, wu_ref, wout_ref, gpost_ref,
     y_ref) = refs[n_a:]
    if n_a == 1:
        a = a_refs[0][...]
    else:
        a = jnp.concatenate([r[...] for r in a_refs], axis=1)
    m = jnp.dot(a, wo_ref[...], preferred_element_type=F32)
    x1 = x_ref[...] + _rms(m, gpm_ref[...])
    h = _rms(x1, gpf_ref[...]).astype(BF16)
    acc = jnp.zeros(x1.shape, F32)
    for c in range(D_FF // FF_CHUNK):
        sl = slice(c * FF_CHUNK, (c + 1) * FF_CHUNK)
        g = jnp.dot(h, wg_ref[:, sl], preferred_element_type=F32)
        u = jnp.dot(h, wu_ref[:, sl], preferred_element_type=F32)
        act = (g * (1.0 / (1.0 + jnp.exp(-g)))) * u
        acc = acc + jnp.dot(act.astype(BF16), wout_ref[sl, :], preferred_element_type=F32)
    y_ref[...] = x1 + _rms(acc, gpost_ref[...])


def _out_ffn(a_list, x, wo, gpm, gpf, wg, wu, wout, gpost):
    N = x.shape[0]
    tn = TOKEN_TILE
    n_a = len(a_list)
    aw = D_MODEL // n_a
    return pl.pallas_call(
        functools.partial(_ffn_kernel, n_a=n_a),
        grid=(N // tn,),
        in_specs=[pl.BlockSpec((tn, aw), lambda i: (i, 0))] * n_a
        + [pl.BlockSpec((tn, D_MODEL), lambda i: (i, 0)),
           _const_spec((D_MODEL, D_MODEL)),
           _const_spec((1, D_MODEL)), _const_spec((1, D_MODEL)),
           _const_spec((D_MODEL, D_FF)), _const_spec((D_MODEL, D_FF)),
           _const_spec((D_FF, D_MODEL)), _const_spec((1, D_MODEL))],
        out_specs=pl.BlockSpec((tn, D_MODEL), lambda i: (i, 0)),
        out_shape=jax.ShapeDtypeStruct((N, D_MODEL), F32),
        compiler_params=_params(1),
        name="out_ffn",
    )(*a_list, x, wo, gpm, gpf, wg, wu, wout, gpost)


def _mla_proj_kernel(x_ref, g_ref, wd_ref, gq_ref, gkv_ref, wq_ref, wqr_ref,
                     wv_ref, wknt_ref, wkrt_ref, ct_ref, st_ref, cost_ref, sint_ref,
                     q_ref, kt_ref, v_ref):
    h = _rms(x_ref[...], g_ref[...]).astype(BF16)
    c = jnp.dot(h, wd_ref[...], preferred_element_type=F32)
    cqn = _rms(c[:, :MLA_Q_RANK], gq_ref[...]).astype(BF16)
    ckvn = _rms(c[:, MLA_Q_RANK:], gkv_ref[...]).astype(BF16)
    ct = jnp.concatenate([ct_ref[...]] * 2, axis=1)
    st = jnp.concatenate([st_ref[...]] * 2, axis=1)
    pw = 2 * MLA_QK_PAD
    for p in range(MLA_HEADS // 2):
        sl = slice(p * pw, (p + 1) * pw)
        a = jnp.dot(cqn, wq_ref[:, sl], preferred_element_type=F32)
        b = jnp.dot(cqn, wqr_ref[:, sl], preferred_element_type=F32)
        q_ref[:, sl] = (a * ct + b * st).astype(BF16)
    v_ref[...] = jnp.dot(ckvn, wv_ref[...], preferred_element_type=F32).astype(BF16)
    knt = lax.dot_general(wknt_ref[...], ckvn, _NT, preferred_element_type=F32)
    kr2 = lax.dot_general(wkrt_ref[...], h, _NT, preferred_element_type=F32)
    krope = (kr2[:MLA_ROPE] * cost_ref[...] + kr2[MLA_ROPE:] * sint_ref[...]).astype(BF16)
    zpad = jnp.zeros((MLA_QK_PAD - MLA_QK, krope.shape[1]), BF16)
    for hh in range(MLA_HEADS):
        r0 = hh * MLA_QK_PAD
        kt_ref[r0:r0 + MLA_NOPE, :] = knt[hh * MLA_NOPE:(hh + 1) * MLA_NOPE].astype(BF16)
        kt_ref[r0 + MLA_NOPE:r0 + MLA_QK, :] = krope
        kt_ref[r0 + MLA_QK:r0 + MLA_QK_PAD, :] = zpad


def _mla_proj(x, g, wd, gq, gkv, wq, wqr, wv, wknt, wkrt, ct, st, cost, sint, B, S):
    N = B * S
    tn = TOKEN_TILE
    spb = S // tn
    qw = MLA_HEADS * MLA_QK_PAD
    vw = MLA_HEADS * MLA_V
    return pl.pallas_call(
        _mla_proj_kernel,
        grid=(N // tn,),
        in_specs=[pl.BlockSpec((tn, D_MODEL), lambda i: (i, 0)),
                  _const_spec((1, D_MODEL)),
                  _const_spec(wd.shape), _const_spec((1, MLA_Q_RANK)),
                  _const_spec((1, MLA_KV_RANK)),
                  _const_spec(wq.shape), _const_spec(wqr.shape), _const_spec(wv.shape),
                  _const_spec(wknt.shape), _const_spec(wkrt.shape),
                  pl.BlockSpec((tn, MLA_QK_PAD), lambda i: (i % spb, 0)),
                  pl.BlockSpec((tn, MLA_QK_PAD), lambda i: (i % spb, 0)),
                  pl.BlockSpec((MLA_ROPE, tn), lambda i: (0, i % spb)),
                  pl.BlockSpec((MLA_ROPE, tn), lambda i: (0, i % spb))],
        out_specs=[pl.BlockSpec((tn, qw), lambda i: (i, 0)),
                   pl.BlockSpec((None, qw, tn), lambda i: (i // spb, 0, i % spb)),
                   pl.BlockSpec((tn, vw), lambda i: (i, 0))],
        out_shape=[jax.ShapeDtypeStruct((N, qw), BF16),
                   jax.ShapeDtypeStruct((B, qw, S), BF16),
                   jax.ShapeDtypeStruct((N, vw), BF16)],
        compiler_params=_params(1),
        name="mla_proj",
    )(x, g, wd, gq, gkv, wq, wqr, wv, wknt, wkrt, ct, st, cost, sint)


def _mla_kernel(q_ref, kt_ref, v_ref, o_ref, m_ref, l_ref, acc_ref, *, S, tq, tk):
    q = q_ref[...]
    qs = (q[:, :MLA_QK_PAD], q[:, MLA_QK_PAD:])
    m_ref[...] = jnp.full(m_ref.shape, -jnp.inf, F32)
    l_ref[...] = jnp.zeros(l_ref.shape, F32)
    acc_ref[...] = jnp.zeros(acc_ref.shape, F32)

    def body(kc, carry):
        k0 = pl.multiple_of(kc * tk, tk)
        kt = kt_ref[:, pl.ds(k0, tk)]
        v = v_ref[pl.ds(k0, tk), :]
        for j in range(2):
            s = jnp.dot(qs[j], kt[j * MLA_QK_PAD:(j + 1) * MLA_QK_PAD, :],
                        preferred_element_type=F32)
            m_prev = m_ref[j]
            m_new = jnp.maximum(m_prev, jnp.max(s, axis=-1, keepdims=True))
            alpha = jnp.exp(m_prev - m_new)
            p = jnp.exp(s - m_new)
            l_ref[j] = alpha * l_ref[j] + jnp.sum(p, axis=-1, keepdims=True)
            acc_ref[j] = alpha * acc_ref[j] + jnp.dot(p.astype(BF16), v,
                                                      preferred_element_type=F32)
            m_ref[j] = m_new
        return carry

    lax.fori_loop(0, S // tk, body, 0)
    o0 = acc_ref[0] * (1.0 / l_ref[0])
    o1 = acc_ref[1] * (1.0 / l_ref[1])
    lane = lax.broadcasted_iota(jnp.int32, o0.shape, 1)
    o_ref[...] = jnp.where(lane < MLA_V, o0, o1).astype(BF16)


def _mla_attn(q, kt, v, B, S):
    N = B * S
    tq, tk = ATT_TQ, min(ATT_TK, S)
    nq = S // tq
    pw = 2 * MLA_QK_PAD
    return pl.pallas_call(
        functools.partial(_mla_kernel, S=S, tq=tq, tk=tk),
        grid=(B, MLA_HEADS // 2, nq),
        in_specs=[pl.BlockSpec((tq, pw), lambda b, p, i: (b * nq + i, p)),
                  pl.BlockSpec((None, pw, S), lambda b, p, i: (b, p, 0)),
                  pl.BlockSpec((S, 2 * MLA_V), lambda b, p, i: (b, p))],
        out_specs=pl.BlockSpec((tq, 2 * MLA_V), lambda b, p, i: (b * nq + i, p)),
        out_shape=jax.ShapeDtypeStruct((N, MLA_HEADS * MLA_V), BF16),
        scratch_shapes=[pltpu.VMEM((2, tq, 1), F32), pltpu.VMEM((2, tq, 1), F32),
                        pltpu.VMEM((2, tq, 2 * MLA_V), F32)],
        compiler_params=_params(3),
        name="mla_attn",
    )(q, kt, v)


def _rot_half_cols(w):
    half = MLA_ROPE // 2
    return jnp.concatenate([-w[..., half:], w[..., :half]], axis=-1)


def _prep_layer0(p):
    w = p['w_in']
    qscale = HEAD_DIM ** -0.5
    o0 = 3 * NA_W
    w_main = jnp.concatenate([w[:, 0:NA_W] * qscale, w[:, NA_W:2 * NA_W], w[:, 2 * NA_W:o0],
                              w[:, o0:o0 + DIFF_W] * (DIFF_QK_DIM ** -0.5),
                              w[:, o0 + 2 * DIFF_W:]], axis=1).astype(BF16)
    wkt = w[:, o0 + DIFF_W:o0 + 2 * DIFF_W].T.astype(BF16)
    lamv = jnp.stack([p['lam_q1'], p['lam_k1'], p['lam_q2'], p['lam_k2']]).astype(F32)
    return dict(w_main=w_main, wkt=wkt, lamv=lamv,
                bias=_na_bias_table(p['rpb']),
                gsub=p['g_sub'].reshape(1, -1).astype(F32))


def _prep_ffn(p):
    w = p['w_ffn_in']
    return dict(wo=p['w_out'].astype(BF16),
                gpm=p['g_post_mix'].reshape(1, -1), gpf=p['g_pre_ffn'].reshape(1, -1),
                wg=w[:, :D_FF].astype(BF16), wu=w[:, D_FF:].astype(BF16),
                wout=p['w_ffn_out'].astype(BF16), gpost=p['g_post_ffn'].reshape(1, -1))


def _prep_layer1(p):
    wd = p['w_down']
    wd_qkv = wd[:, :MLA_Q_RANK + MLA_KV_RANK].astype(BF16)
    kr = wd[:, MLA_Q_RANK + MLA_KV_RANK:]
    wkrt = jnp.concatenate([kr, _rot_half_cols(kr)], axis=1).T.astype(BF16)
    wq3 = p['w_uq'].reshape(MLA_Q_RANK, MLA_HEADS, MLA_QK)
    zq = jnp.zeros((MLA_Q_RANK, MLA_HEADS, MLA_QK_PAD - MLA_QK), F32)
    wq = jnp.concatenate([wq3, zq], axis=-1).reshape(MLA_Q_RANK, -1).astype(BF16)
    zn = jnp.zeros((MLA_Q_RANK, MLA_HEADS, MLA_NOPE), F32)
    wqr = jnp.concatenate([zn, _rot_half_cols(wq3[..., MLA_NOPE:]), zq],
                          axis=-1).reshape(MLA_Q_RANK, -1).astype(BF16)
    wkv3 = p['w_ukv'].reshape(MLA_KV_RANK, MLA_HEADS, MLA_NOPE + MLA_V)
    wknt = wkv3[..., :MLA_NOPE].reshape(MLA_KV_RANK, -1).T.astype(BF16)
    wv = wkv3[..., MLA_NOPE:].reshape(MLA_KV_RANK, -1).astype(BF16)
    return dict(wd=wd_qkv, wkrt=wkrt, wq=wq, wqr=wqr, wknt=wknt, wv=wv,
                g=p['g_pre_mix'].reshape(1, -1), gq=p['g_q'].reshape(1, -1),
                gkv=p['g_kv'].reshape(1, -1))


def _rope_tables(S):
    half = MLA_ROPE // 2
    inv = jnp.asarray(ROPE_THETA ** (-np.arange(half, dtype=np.float32) * 2.0 / MLA_ROPE), F32)
    ang = jnp.arange(S).astype(F32)[:, None] * inv[None, :]
    cos2 = jnp.concatenate([jnp.cos(ang)] * 2, axis=1)
    sin2 = jnp.concatenate([jnp.sin(ang)] * 2, axis=1)
    scale = MLA_QK ** -0.5
    ones = jnp.ones((S, MLA_NOPE), F32)
    z_nope = jnp.zeros((S, MLA_NOPE), F32)
    z_pad = jnp.zeros((S, MLA_QK_PAD - MLA_QK), F32)
    ct = jnp.concatenate([ones, cos2, z_pad], axis=1) * scale
    st = jnp.concatenate([z_nope, sin2, z_pad], axis=1) * scale
    return ct, st, cos2.T, sin2.T


def _trunk(x3, p0, f0, p1, f1, g0):
    B, S, _ = x3.shape
    x = x3.reshape(B * S, D_MODEL)
    lam_init0 = 0.8 - 0.6 * math.exp(-0.3 * 0)
    naq, nak, nav, dq, dv, dkt = _proj0(x, g0, p0['w_main'], p0['wkt'], B, S)
    a = _na_attn(naq, nak, nav, p0['bias'], B, S)
    b = _diff_attn(dq, dkt, dv, p0['lamv'], p0['gsub'], B, S, lam_init0)
    x = _out_ffn([a, b], x, f0['wo'], f0['gpm'], f0['gpf'], f0['wg'], f0['wu'],
                 f0['wout'], f0['gpost'])
    ct, st, cost, sint = _rope_tables(S)
    q, kt, v = _mla_proj(x, p1['g'], p1['wd'], p1['gq'], p1['gkv'], p1['wq'], p1['wqr'],
                         p1['wv'], p1['wknt'], p1['wkrt'], ct, st, cost, sint, B, S)
    o = _mla_attn(q, kt, v, B, S)
    x = _out_ffn([o], x, f1['wo'], f1['gpm'], f1['gpf'], f1['wg'], f1['wu'],
                 f1['wout'], f1['gpost'])
    return x.reshape(B, S, D_MODEL)


def kernel(x_prompt, x_sample, l0_g_pre_mix, l0_w_in, l0_rpb, l0_lam_q1, l0_lam_k1, l0_lam_q2, l0_lam_k2, l0_g_sub, l0_w_out, l0_g_post_mix, l0_g_pre_ffn, l0_w_ffn_in, l0_w_ffn_out, l0_g_post_ffn, l1_g_pre_mix, l1_w_down, l1_g_q, l1_g_kv, l1_w_uq, l1_w_ukv, l1_w_out, l1_g_post_mix, l1_g_pre_ffn, l1_w_ffn_in, l1_w_ffn_out, l1_g_post_ffn):
    p0 = _prep_layer0(dict(w_in=l0_w_in, rpb=l0_rpb, lam_q1=l0_lam_q1, lam_k1=l0_lam_k1,
                           lam_q2=l0_lam_q2, lam_k2=l0_lam_k2, g_sub=l0_g_sub))
    f0 = _prep_ffn(dict(w_out=l0_w_out, g_post_mix=l0_g_post_mix, g_pre_ffn=l0_g_pre_ffn,
                        w_ffn_in=l0_w_ffn_in, w_ffn_out=l0_w_ffn_out, g_post_ffn=l0_g_post_ffn))
    p1 = _prep_layer1(dict(g_pre_mix=l1_g_pre_mix, w_down=l1_w_down, g_q=l1_g_q, g_kv=l1_g_kv,
                           w_uq=l1_w_uq, w_ukv=l1_w_ukv))
    f1 = _prep_ffn(dict(w_out=l1_w_out, g_post_mix=l1_g_post_mix, g_pre_ffn=l1_g_pre_ffn,
                        w_ffn_in=l1_w_ffn_in, w_ffn_out=l1_w_ffn_out, g_post_ffn=l1_g_post_ffn))
    g0 = l0_g_pre_mix.reshape(1, -1)
    y_prompt = _trunk(x_prompt, p0, f0, p1, f1, g0)
    y_sample = _trunk(x_sample, p0, f0, p1, f1, g0)
    return (y_prompt, y_sample)
```

```python
import functools
import math

import numpy as np
import jax
import jax.numpy as jnp
from jax import lax
from jax.experimental import pallas as pl
from jax.experimental.pallas import tpu as pltpu

F32 = jnp.float32
BF16 = jnp.bfloat16

D_MODEL = 1024
GRID_W = 64
HEAD_DIM = 64
EPS = 1e-6

NA_HEADS = 8
NA_WIN_R = 8
NA_WIN_C = 16
NA_W = NA_HEADS * HEAD_DIM
NA_ROWS_PER_STEP = 8
NA_KEY_ROWS = 16
NA_KEY_BLOCK_ROWS = 4

DIFF_HEADS = 4
DIFF_QK_DIM = 64
DIFF_V_DIM = 128
DIFF_W = 512

MLA_HEADS = 16
MLA_Q_RANK = 384
MLA_KV_RANK = 256
MLA_NOPE = 64
MLA_ROPE = 32
MLA_V = 64
MLA_QK = MLA_NOPE + MLA_ROPE
MLA_QK_PAD = 128
ROPE_THETA = 10000.0

D_FF = 2816
FF_CHUNK = 256

VMEM_LIMIT = 56 * 1024 * 1024

TOKEN_TILE = 512
ATT_TQ = 512
ATT_TKC = 512
ONES_PAD = 16
DIFF_VT = DIFF_V_DIM + ONES_PAD
MLA_VT = MLA_V + ONES_PAD
LOG2E = math.log2(math.e)

_NT = (((1,), (1,)), ((), ()))
_TN = (((0,), (0,)), ((), ()))


def _rms(x, g):
    ms = jnp.mean(x * x, axis=-1, keepdims=True)
    return (x * lax.rsqrt(ms + EPS)) * g


def _const_spec(shape):
    nd = len(shape)
    return pl.BlockSpec(shape, lambda *_: (0,) * nd, pipeline_mode=pl.Buffered(1))


def _params(n_axes):
    return pltpu.CompilerParams(
        dimension_semantics=("parallel",) * n_axes, vmem_limit_bytes=VMEM_LIMIT)


def _with_ones_rows(vt, per_head, data_rows):
    row = lax.broadcasted_iota(jnp.int32, vt.shape, 0)
    return jnp.where(row % per_head == data_rows, 1.0, vt)


def _proj0_kernel(x_ref, g_ref, w_ref, wt_ref,
                  naq_ref, nak_ref, nav_ref, dk_ref, dqt_ref, dvt_ref):
    h = _rms(x_ref[...], g_ref[...]).astype(BF16)
    for i, o_ref in enumerate((naq_ref, nak_ref, nav_ref, dk_ref)):
        o_ref[...] = jnp.dot(h, w_ref[:, i * 512:(i + 1) * 512],
                             preferred_element_type=F32).astype(BF16)
    t = lax.dot_general(wt_ref[...], h, _NT, preferred_element_type=F32)
    dqt_ref[...] = (t[:DIFF_W] * LOG2E).astype(BF16)
    dvt_ref[...] = _with_ones_rows(t[DIFF_W:], DIFF_VT, DIFF_V_DIM).astype(BF16)


def _proj0(x, g, w, wt, B, S):
    N = B * S
    tn = TOKEN_TILE
    spb = S // tn
    tok = pl.BlockSpec((tn, 512), lambda i: (i, 0))
    nvt = DIFF_HEADS * DIFF_VT
    return pl.pallas_call(
        _proj0_kernel,
        grid=(N // tn,),
        in_specs=[pl.BlockSpec((tn, D_MODEL), lambda i: (i, 0)),
                  _const_spec((1, D_MODEL)),
                  _const_spec(w.shape),
                  _const_spec(wt.shape)],
        out_specs=[tok, tok, tok, tok,
                   pl.BlockSpec((None, DIFF_W, tn), lambda i: (i // spb, 0, i % spb)),
                   pl.BlockSpec((None, nvt, tn), lambda i: (i // spb, 0, i % spb))],
        out_shape=[jax.ShapeDtypeStruct((N, 512), BF16)] * 4
        + [jax.ShapeDtypeStruct((B, DIFF_W, S), BF16),
           jax.ShapeDtypeStruct((B, nvt, S), BF16)],
        compiler_params=_params(1),
        name="proj0",
    )(x, g, w, wt)


def _na_kernel(q_ref, k0, k1, k2, k3, v0, v1, v2, v3, bias_ref, o_ref,
               kbuf, vbuf, *, rows):
    rb = pl.program_id(1)
    blk = NA_KEY_BLOCK_ROWS * GRID_W
    for i, (kr, vr) in enumerate(((k0, v0), (k1, v1), (k2, v2), (k3, v3))):
        kbuf[i * blk:(i + 1) * blk, :] = kr[...]
        vbuf[i * blk:(i + 1) * blk, :] = vr[...]
    kw0 = jnp.clip(rb * NA_ROWS_PER_STEP - NA_WIN_R // 2, 0, rows - NA_KEY_ROWS)
    nkeys = NA_WIN_R * GRID_W

    def row_body(rl, carry):
        r = rb * NA_ROWS_PER_STEP + rl
        rs = jnp.clip(r - NA_WIN_R // 2, 0, rows - NA_WIN_R)
        off = pl.multiple_of((rs - kw0) * GRID_W, GRID_W)
        dr0 = rs - r + NA_WIN_R - 1
        qoff = pl.multiple_of(rl * GRID_W, GRID_W)
        qrow = q_ref[pl.ds(qoff, GRID_W), :]
        kwin = kbuf[pl.ds(off, nkeys), :]
        vwin = vbuf[pl.ds(off, nkeys), :]
        outs = []
        for h in range(NA_HEADS):
            sl = slice(h * HEAD_DIM, (h + 1) * HEAD_DIM)
            s = lax.dot_general(qrow[:, sl], kwin[:, sl], _NT,
                                preferred_element_type=F32)
            s = s + bias_ref[dr0, h]
            m = jnp.max(s, axis=-1, keepdims=True)
            p = jnp.exp(s - m)
            l = jnp.sum(p, axis=-1, keepdims=True)
            o = jnp.dot(p.astype(BF16), vwin[:, sl], preferred_element_type=F32)
            outs.append(o * (1.0 / l))
        o_ref[pl.ds(qoff, GRID_W), :] = jnp.concatenate(outs, axis=1).astype(BF16)
        return carry

    lax.fori_loop(0, NA_ROWS_PER_STEP, row_body, 0)


def _na_attn(q, k, v, bias, B, S):
    rows = S // GRID_W
    assert rows % NA_ROWS_PER_STEP == 0 and rows >= NA_KEY_ROWS
    N = B * S
    tq = NA_ROWS_PER_STEP * GRID_W
    blk = NA_KEY_BLOCK_ROWS * GRID_W
    bpb = rows // NA_KEY_BLOCK_ROWS

    def kv_spec(i):
        def idx(b, rb):
            kw0 = jnp.clip(rb * NA_ROWS_PER_STEP - NA_WIN_R // 2, 0, rows - NA_KEY_ROWS)
            return (b * bpb + kw0 // NA_KEY_BLOCK_ROWS + i, 0)
        return pl.BlockSpec((blk, NA_W), idx)

    kv_specs = [kv_spec(i) for i in range(NA_KEY_ROWS // NA_KEY_BLOCK_ROWS)]
    qspec = pl.BlockSpec((tq, NA_W), lambda b, rb: (b * (rows // NA_ROWS_PER_STEP) + rb, 0))
    return pl.pallas_call(
        functools.partial(_na_kernel, rows=rows),
        grid=(B, rows // NA_ROWS_PER_STEP),
        in_specs=[qspec] + kv_specs + kv_specs + [_const_spec(bias.shape)],
        out_specs=qspec,
        out_shape=jax.ShapeDtypeStruct((N, NA_W), BF16),
        scratch_shapes=[pltpu.VMEM((NA_KEY_ROWS * GRID_W, NA_W), BF16),
                        pltpu.VMEM((NA_KEY_ROWS * GRID_W, NA_W), BF16)],
        compiler_params=_params(2),
        name="na_attn",
    )(q, k, k, k, k, v, v, v, v, bias)


def _na_bias_table(rpb):
    c = np.arange(GRID_W)[:, None]
    cp = np.arange(GRID_W)[None, :]
    win0 = np.clip(c - NA_WIN_C // 2, 0, GRID_W - NA_WIN_C)
    ok = (cp >= win0) & (cp < win0 + NA_WIN_C)
    cidx = np.clip(cp - c + NA_WIN_C - 1, 0, 2 * NA_WIN_C - 2)
    t = rpb.astype(F32)[:, :, cidx]
    t = jnp.where(jnp.asarray(ok)[None, None], t, -jnp.inf)
    per = [t[:, d:d + NA_WIN_R].transpose(0, 2, 1, 3).reshape(NA_HEADS, GRID_W, NA_WIN_R * GRID_W)
           for d in range(NA_WIN_R)]
    return jnp.stack(per, axis=0)


def _flash_pair(n_chunks, qk, sm_pv, init):
    def body(i, carry):
        c = 2 * i
        qk(1, c + 1)
        carry = sm_pv(0, c, carry)
        qk(0, c + 2)
        return sm_pv(1, c + 1, carry)

    qk(0, 0)
    carry = lax.fori_loop(0, n_chunks // 2 - 1, body, init)
    qk(1, n_chunks - 1)
    carry = sm_pv(0, n_chunks - 2, carry)
    return sm_pv(1, n_chunks - 1, carry)


def _online_softmax_step(s, m_prev, a_prev, vt_chunk):
    m_new = jnp.maximum(m_prev, jnp.max(s, axis=0, keepdims=True))
    alpha = jnp.exp2(m_prev - m_new)
    p = jnp.exp2(s - m_new).astype(BF16)
    a_new = alpha * a_prev + jnp.dot(vt_chunk, p, preferred_element_type=F32)
    return m_new, a_new


def _diff_kernel(slopes_ref, qt_ref, k_ref, vt_ref, lamv_ref, gsub_ref, o_ref, s_scr,
                 *, S, tq, tkc, hp, lam_init):
    g = pl.program_id(1)
    qi = pl.program_id(2)
    row = lax.broadcasted_iota(jnp.int32, (tkc, tq), 0)
    col = lax.broadcasted_iota(jnp.int32, (tkc, tq), 1)
    base = (col - row + qi * tq).astype(F32)
    lv = lamv_ref[...]
    lam = (jnp.exp(jnp.sum(lv[0:1] * lv[1:2], axis=-1, keepdims=True))
           - jnp.exp(jnp.sum(lv[2:3] * lv[3:4], axis=-1, keepdims=True)) + lam_init)
    zeros = jnp.zeros((DIFF_QK_DIM, tq), BF16)

    for hh in range(hp):
        slope = slopes_ref[g * hp + hh]
        ksl = slice(hh * 128, (hh + 1) * 128)
        qpair = qt_ref[ksl, :]
        qts = (jnp.concatenate([qpair[:DIFF_QK_DIM], zeros], axis=0),
               jnp.concatenate([zeros, qpair[DIFF_QK_DIM:]], axis=0))

        def qk(slot, c):
            k0 = pl.multiple_of(c * tkc, tkc)
            kc = k_ref[pl.ds(k0, tkc), ksl]
            for j in range(2):
                s_scr[slot, j] = jnp.dot(kc, qts[j], preferred_element_type=F32)

        def sm_pv(slot, c, carry):
            k0 = pl.multiple_of(c * tkc, tkc)
            bias = slope * jnp.abs(base - jnp.asarray(c * tkc).astype(F32))
            vc = vt_ref[hh * DIFF_VT:(hh + 1) * DIFF_VT, pl.ds(k0, tkc)]
            out = []
            for j in range(2):
                m_new, a_new = _online_softmax_step(
                    s_scr[slot, j] - bias, carry[2 * j], carry[2 * j + 1], vc)
                out += [m_new, a_new]
            return tuple(out)

        init = (jnp.full((1, tq), -jnp.inf, F32), jnp.zeros((DIFF_VT, tq), F32)) * 2
        _, a0, _, a1 = _flash_pair(S // tkc, qk, sm_pv, init)
        o0 = a0[:DIFF_V_DIM] * (1.0 / a0[DIFF_V_DIM:DIFF_V_DIM + 1])
        o1 = a1[:DIFF_V_DIM] * (1.0 / a1[DIFF_V_DIM:DIFF_V_DIM + 1])
        o = o0 - lam * o1
        ms = jnp.mean(o * o, axis=0, keepdims=True)
        y = (o * lax.rsqrt(ms + EPS)) * gsub_ref[...]
        o_ref[ksl, :] = (y * (1.0 - lam_init)).astype(BF16)


def _diff_attn(qt, k, vt, lamv, gsub, B, S, lam_init):
    tq, tkc = ATT_TQ, ATT_TKC
    hp = 2 if S <= 4096 else 1
    assert S % tq == 0 and (S // tkc) % 2 == 0 and DIFF_HEADS % hp == 0
    nq = S // tq
    slopes = jnp.asarray(
        2.0 ** (-8.0 * np.arange(1, DIFF_HEADS + 1) / DIFF_HEADS) * LOG2E, F32)
    return pl.pallas_call(
        functools.partial(_diff_kernel, S=S, tq=tq, tkc=tkc, hp=hp, lam_init=lam_init),
        grid=(B, DIFF_HEADS // hp, nq),
        in_specs=[pl.BlockSpec(memory_space=pltpu.SMEM),
                  pl.BlockSpec((None, hp * 128, tq), lambda b, g, i: (b, g, i)),
                  pl.BlockSpec((S, hp * 128), lambda b, g, i: (b, g)),
                  pl.BlockSpec((None, hp * DIFF_VT, S), lambda b, g, i: (b, g, 0)),
                  _const_spec((4, DIFF_QK_DIM)),
                  _const_spec((DIFF_V_DIM, 1))],
        out_specs=pl.BlockSpec((None, hp * 128, tq), lambda b, g, i: (b, g, i)),
        out_shape=jax.ShapeDtypeStruct((B, DIFF_W, S), BF16),
        scratch_shapes=[pltpu.VMEM((2, 2, tkc, tq), F32)],
        compiler_params=_params(3),
        name="diff_attn",
    )(slopes, qt, k, vt, lamv, gsub)


def _ffn_kernel(*refs, n_tok):
    a_refs = refs[:n_tok + 1]
    (x_ref, wo_ref, gpm_ref, gpf_ref, wg_ref, wu_ref, wout_ref, gpost_ref,
     y_ref) = refs[n_tok + 1:]
    m = None
    r0 = 0
    for a_ref in a_refs[:n_tok]:
        w = a_ref.shape[1]
        t = jnp.dot(a_ref[...], wo_ref[r0:r0 + w, :], preferred_element_type=F32)
        m = t if m is None else m + t
        r0 += w
    at_ref = a_refs[n_tok]
    t = lax.dot_general(at_ref[...], wo_ref[r0:r0 + at_ref.shape[0], :], _TN,
                        preferred_element_type=F32)
    m = t if m is None else m + t
    x1 = x_ref[...] + _rms(m, gpm_ref[...])
    h = _rms(x1, gpf_ref[...]).astype(BF16)
    acc = jnp.zeros(x1.shape, F32)
    for c in range(D_FF // FF_CHUNK):
        sl = slice(c * FF_CHUNK, (c + 1) * FF_CHUNK)
        g = jnp.dot(h, wg_ref[:, sl], preferred_element_type=F32)
        u = jnp.dot(h, wu_ref[:, sl], preferred_element_type=F32)
        act = (g * (1.0 / (1.0 + jnp.exp(-g)))) * u
        acc = acc + jnp.dot(act.astype(BF16), wout_ref[sl, :], preferred_element_type=F32)
    y_ref[...] = x1 + _rms(acc, gpost_ref[...])


def _out_ffn(a_tok, a_t, x, wo, gpm, gpf, wg, wu, wout, gpost, S):
    N = x.shape[0]
    tn = TOKEN_TILE
    spb = S // tn
    wt = a_t.shape[1]
    return pl.pallas_call(
        functools.partial(_ffn_kernel, n_tok=len(a_tok)),
        grid=(N // tn,),
        in_specs=[pl.BlockSpec((tn, a.shape[1]), lambda i: (i, 0)) for a in a_tok]
        + [pl.BlockSpec((None, wt, tn), lambda i: (i // spb, 0, i % spb)),
           pl.BlockSpec((tn, D_MODEL), lambda i: (i, 0)),
           _const_spec((D_MODEL, D_MODEL)),
           _const_spec((1, D_MODEL)), _const_spec((1, D_MODEL)),
           _const_spec((D_MODEL, D_FF)), _const_spec((D_MODEL, D_FF)),
           _const_spec((D_FF, D_MODEL)), _const_spec((1, D_MODEL))],
        out_specs=pl.BlockSpec((tn, D_MODEL), lambda i: (i, 0)),
        out_shape=jax.ShapeDtypeStruct((N, D_MODEL), F32),
        compiler_params=_params(1),
        name="out_ffn",
    )(*a_tok, a_t, x, wo, gpm, gpf, wg, wu, wout, gpost)


MLA_Q_GROUP = 4


def _mla_proj_kernel(x_ref, g_ref, wd_ref, gq_ref, gkv_ref, wqt_ref, wqrt_ref,
                     wkn_ref, wvt_ref, ctt_ref, stt_ref, ck_ref, sk_ref,
                     qt_ref, k_ref, vt_ref):
    h = _rms(x_ref[...], g_ref[...]).astype(BF16)
    c = jnp.dot(h, wd_ref[...], preferred_element_type=F32)
    cqn = _rms(c[:, :MLA_Q_RANK], gq_ref[...]).astype(BF16)
    o1 = MLA_Q_RANK + MLA_KV_RANK
    ckvn = _rms(c[:, MLA_Q_RANK:o1], gkv_ref[...]).astype(BF16)
    krope = c[:, o1:o1 + MLA_QK_PAD] * ck_ref[...] + c[:, o1 + MLA_QK_PAD:] * sk_ref[...]

    gw = MLA_Q_GROUP * MLA_QK_PAD
    ctt = jnp.concatenate([ctt_ref[...]] * MLA_Q_GROUP, axis=0)
    stt = jnp.concatenate([stt_ref[...]] * MLA_Q_GROUP, axis=0)
    for p in range(MLA_HEADS // MLA_Q_GROUP):
        sl = slice(p * gw, (p + 1) * gw)
        a = lax.dot_general(wqt_ref[sl, :], cqn, _NT, preferred_element_type=F32)
        b = lax.dot_general(wqrt_ref[sl, :], cqn, _NT, preferred_element_type=F32)
        qt_ref[sl, :] = (a * ctt + b * stt).astype(BF16)

    for p in range(MLA_HEADS // MLA_Q_GROUP):
        sl = slice(p * gw, (p + 1) * gw)
        kn = jnp.dot(ckvn, wkn_ref[:, sl], preferred_element_type=F32)
        k_ref[:, sl] = (kn + jnp.concatenate([krope] * MLA_Q_GROUP, axis=1)).astype(BF16)

    vt = lax.dot_general(wvt_ref[...], ckvn, _NT, preferred_element_type=F32)
    vt_ref[...] = _with_ones_rows(vt, MLA_VT, MLA_V).astype(BF16)


def _mla_proj(x, p1, tables, B, S):
    N = B * S
    tn = TOKEN_TILE
    spb = S // tn
    qw = MLA_HEADS * MLA_QK_PAD
    nvt = MLA_HEADS * MLA_VT
    ctt, stt, ck, sk = tables
    return pl.pallas_call(
        _mla_proj_kernel,
        grid=(N // tn,),
        in_specs=[pl.BlockSpec((tn, D_MODEL), lambda i: (i, 0)),
                  _const_spec((1, D_MODEL)),
                  _const_spec(p1['wd'].shape), _const_spec((1, MLA_Q_RANK)),
                  _const_spec((1, MLA_KV_RANK)),
                  _const_spec(p1['wqt'].shape), _const_spec(p1['wqrt'].shape),
                  _const_spec(p1['wkn'].shape), _const_spec(p1['wvt'].shape),
                  pl.BlockSpec((MLA_QK_PAD, tn), lambda i: (0, i % spb)),
                  pl.BlockSpec((MLA_QK_PAD, tn), lambda i: (0, i % spb)),
                  pl.BlockSpec((tn, MLA_QK_PAD), lambda i: (i % spb, 0)),
                  pl.BlockSpec((tn, MLA_QK_PAD), lambda i: (i % spb, 0))],
        out_specs=[pl.BlockSpec((None, qw, tn), lambda i: (i // spb, 0, i % spb)),
                   pl.BlockSpec((tn, qw), lambda i: (i, 0)),
                   pl.BlockSpec((None, nvt, tn), lambda i: (i // spb, 0, i % spb))],
        out_shape=[jax.ShapeDtypeStruct((B, qw, S), BF16),
                   jax.ShapeDtypeStruct((N, qw), BF16),
                   jax.ShapeDtypeStruct((B, nvt, S), BF16)],
        compiler_params=_params(1),
        name="mla_proj",
    )(x, p1['g'], p1['wd'], p1['gq'], p1['gkv'], p1['wqt'], p1['wqrt'],
      p1['wkn'], p1['wvt'], ctt, stt, ck, sk)


def _mla_kernel(qt_ref, k_ref, vt_ref, o_ref, s_scr, *, S, tq, tkc, hp):
    for h0 in range(0, hp, 2):
        heads = (h0, h0 + 1)
        qts = [qt_ref[h * MLA_QK_PAD:(h + 1) * MLA_QK_PAD, :] for h in heads]

        def qk(slot, c):
            k0 = pl.multiple_of(c * tkc, tkc)
            for j, h in enumerate(heads):
                kc = k_ref[pl.ds(k0, tkc), h * MLA_QK_PAD:(h + 1) * MLA_QK_PAD]
                s_scr[slot, j] = jnp.dot(kc, qts[j], preferred_element_type=F32)

        def sm_pv(slot, c, carry):
            k0 = pl.multiple_of(c * tkc, tkc)
            out = []
            for j, h in enumerate(heads):
                vc = vt_ref[h * MLA_VT:(h + 1) * MLA_VT, pl.ds(k0, tkc)]
                m_new, a_new = _online_softmax_step(
                    s_scr[slot, j], carry[2 * j], carry[2 * j + 1], vc)
                out += [m_new, a_new]
            return tuple(out)

        init = (jnp.full((1, tq), -jnp.inf, F32), jnp.zeros((MLA_VT, tq), F32)) * 2
        res = _flash_pair(S // tkc, qk, sm_pv, init)
        for j, h in enumerate(heads):
            a = res[2 * j + 1]
            o_ref[h * MLA_V:(h + 1) * MLA_V, :] = (
                a[:MLA_V] * (1.0 / a[MLA_V:MLA_V + 1])).astype(BF16)


def _mla_attn(qt, k, vt, B, S):
    tq, tkc = ATT_TQ, ATT_TKC
    hp = 4 if S <= 4096 else 2
    assert S % tq == 0 and (S // tkc) % 2 == 0 and MLA_HEADS % hp == 0
    nq = S // tq
    return pl.pallas_call(
        functools.partial(_mla_kernel, S=S, tq=tq, tkc=tkc, hp=hp),
        grid=(B, MLA_HEADS // hp, nq),
        in_specs=[pl.BlockSpec((None, hp * MLA_QK_PAD, tq), lambda b, g, i: (b, g, i)),
                  pl.BlockSpec((S, hp * MLA_QK_PAD), lambda b, g, i: (b, g)),
                  pl.BlockSpec((None, hp * MLA_VT, S), lambda b, g, i: (b, g, 0))],
        out_specs=pl.BlockSpec((None, hp * MLA_V, tq), lambda b, g, i: (b, g, i)),
        out_shape=jax.ShapeDtypeStruct((B, MLA_HEADS * MLA_V, S), BF16),
        scratch_shapes=[pltpu.VMEM((2, 2, tkc, tq), F32)],
        compiler_params=_params(3),
        name="mla_attn",
    )(qt, k, vt)


def _rot_half_cols(w):
    half = MLA_ROPE // 2
    return jnp.concatenate([-w[..., half:], w[..., :half]], axis=-1)


def _prep_layer0(p):
    w = p['w_in']
    qscale = HEAD_DIM ** -0.5
    o0 = 3 * NA_W
    w_main = jnp.concatenate([w[:, 0:NA_W] * qscale, w[:, NA_W:2 * NA_W], w[:, 2 * NA_W:o0],
                              w[:, o0 + DIFF_W:o0 + 2 * DIFF_W]], axis=1).astype(BF16)
    wq_t = (w[:, o0:o0 + DIFF_W] * (DIFF_QK_DIM ** -0.5)).T
    wv3 = w[:, o0 + 2 * DIFF_W:].T.reshape(DIFF_HEADS, DIFF_V_DIM, D_MODEL)
    wv_t = jnp.concatenate([wv3, jnp.zeros((DIFF_HEADS, ONES_PAD, D_MODEL), F32)],
                           axis=1).reshape(DIFF_HEADS * DIFF_VT, D_MODEL)
    wt = jnp.concatenate([wq_t, wv_t], axis=0).astype(BF16)
    lamv = jnp.stack([p['lam_q1'], p['lam_k1'], p['lam_q2'], p['lam_k2']]).astype(F32)
    return dict(w_main=w_main, wt=wt, lamv=lamv,
                bias=_na_bias_table(p['rpb']),
                gsub=p['g_sub'].reshape(-1, 1).astype(F32))


def _prep_ffn(p):
    w = p['w_ffn_in']
    return dict(wo=p['w_out'].astype(BF16),
                gpm=p['g_post_mix'].reshape(1, -1), gpf=p['g_pre_ffn'].reshape(1, -1),
                wg=w[:, :D_FF].astype(BF16), wu=w[:, D_FF:].astype(BF16),
                wout=p['w_ffn_out'].astype(BF16), gpost=p['g_post_ffn'].reshape(1, -1))


def _prep_layer1(p):
    wd = p['w_down']
    o1 = MLA_Q_RANK + MLA_KV_RANK
    kr = wd[:, o1:]
    zl = jnp.zeros((D_MODEL, MLA_NOPE), F32)
    zr = jnp.zeros((D_MODEL, MLA_QK_PAD - MLA_QK), F32)
    wd_all = jnp.concatenate([wd[:, :o1], zl, kr, zr, zl, _rot_half_cols(kr), zr],
                             axis=1).astype(BF16)
    wq3 = p['w_uq'].reshape(MLA_Q_RANK, MLA_HEADS, MLA_QK)
    zq = jnp.zeros((MLA_Q_RANK, MLA_HEADS, MLA_QK_PAD - MLA_QK), F32)
    wq = jnp.concatenate([wq3, zq], axis=-1).reshape(MLA_Q_RANK, -1)
    zn = jnp.zeros((MLA_Q_RANK, MLA_HEADS, MLA_NOPE), F32)
    wqr = jnp.concatenate([zn, _rot_half_cols(wq3[..., MLA_NOPE:]), zq],
                          axis=-1).reshape(MLA_Q_RANK, -1)
    wkv3 = p['w_ukv'].reshape(MLA_KV_RANK, MLA_HEADS, MLA_NOPE + MLA_V)
    zk = jnp.zeros((MLA_KV_RANK, MLA_HEADS, MLA_QK_PAD - MLA_NOPE), F32)
    wkn = jnp.concatenate([wkv3[..., :MLA_NOPE], zk], axis=-1).reshape(MLA_KV_RANK, -1)
    wv3 = wkv3[..., MLA_NOPE:].transpose(1, 2, 0)
    wvt = jnp.concatenate([wv3, jnp.zeros((MLA_HEADS, ONES_PAD, MLA_KV_RANK), F32)],
                          axis=1).reshape(MLA_HEADS * MLA_VT, MLA_KV_RANK)
    return dict(wd=wd_all, wqt=wq.T.astype(BF16), wqrt=wqr.T.astype(BF16),
                wkn=wkn.astype(BF16), wvt=wvt.astype(BF16),
                g=p['g_pre_mix'].reshape(1, -1), gq=p['g_q'].reshape(1, -1),
                gkv=p['g_kv'].reshape(1, -1))


def _rope_tables(S):
    half = MLA_ROPE // 2
    inv = jnp.asarray(ROPE_THETA ** (-np.arange(half, dtype=np.float32) * 2.0 / MLA_ROPE), F32)
    ang = jnp.arange(S).astype(F32)[:, None] * inv[None, :]
    cos2 = jnp.concatenate([jnp.cos(ang)] * 2, axis=1)
    sin2 = jnp.concatenate([jnp.sin(ang)] * 2, axis=1)
    ones = jnp.ones((S, MLA_NOPE), F32)
    z_nope = jnp.zeros((S, MLA_NOPE), F32)
    z_pad = jnp.zeros((S, MLA_QK_PAD - MLA_QK), F32)
    ck = jnp.concatenate([z_nope, cos2, z_pad], axis=1)
    sk = jnp.concatenate([z_nope, sin2, z_pad], axis=1)
    qscale = MLA_QK ** -0.5 * LOG2E
    ctt = (jnp.concatenate([ones, cos2, z_pad], axis=1) * qscale).T
    stt = (sk * qscale).T
    return ctt, stt, ck, sk


def _trunk(x3, p0, f0, p1, f1, g0):
    B, S, _ = x3.shape
    x = x3.reshape(B * S, D_MODEL)
    lam_init0 = 0.8 - 0.6 * math.exp(-0.3 * 0)
    naq, nak, nav, dk, dqt, dvt = _proj0(x, g0, p0['w_main'], p0['wt'], B, S)
    a = _na_attn(naq, nak, nav, p0['bias'], B, S)
    bt = _diff_attn(dqt, dk, dvt, p0['lamv'], p0['gsub'], B, S, lam_init0)
    x = _out_ffn([a], bt, x, f0['wo'], f0['gpm'], f0['gpf'], f0['wg'], f0['wu'],
                 f0['wout'], f0['gpost'], S)
    qt, k, vt = _mla_proj(x, p1, _rope_tables(S), B, S)
    ot = _mla_attn(qt, k, vt, B, S)
    x = _out_ffn([], ot, x, f1['wo'], f1['gpm'], f1['gpf'], f1['wg'], f1['wu'],
                 f1['wout'], f1['gpost'], S)
    return x.reshape(B, S, D_MODEL)


def kernel(x_prompt, x_sample, l0_g_pre_mix, l0_w_in, l0_rpb, l0_lam_q1, l0_lam_k1, l0_lam_q2, l0_lam_k2, l0_g_sub, l0_w_out, l0_g_post_mix, l0_g_pre_ffn, l0_w_ffn_in, l0_w_ffn_out, l0_g_post_ffn, l1_g_pre_mix, l1_w_down, l1_g_q, l1_g_kv, l1_w_uq, l1_w_ukv, l1_w_out, l1_g_post_mix, l1_g_pre_ffn, l1_w_ffn_in, l1_w_ffn_out, l1_g_post_ffn):
    p0 = _prep_layer0(dict(w_in=l0_w_in, rpb=l0_rpb, lam_q1=l0_lam_q1, lam_k1=l0_lam_k1,
                           lam_q2=l0_lam_q2, lam_k2=l0_lam_k2, g_sub=l0_g_sub))
    f0 = _prep_ffn(dict(w_out=l0_w_out, g_post_mix=l0_g_post_mix, g_pre_ffn=l0_g_pre_ffn,
                        w_ffn_in=l0_w_ffn_in, w_ffn_out=l0_w_ffn_out, g_post_ffn=l0_g_post_ffn))
    p1 = _prep_layer1(dict(g_pre_mix=l1_g_pre_mix, w_down=l1_w_down, g_q=l1_g_q, g_kv=l1_g_kv,
                           w_uq=l1_w_uq, w_ukv=l1_w_ukv))
    f1 = _prep_ffn(dict(w_out=l1_w_out, g_post_mix=l1_g_post_mix, g_pre_ffn=l1_g_pre_ffn,
                        w_ffn_in=l1_w_ffn_in, w_ffn_out=l1_w_ffn_out, g_post_ffn=l1_g_post_ffn))
    g0 = l0_g_pre_mix.reshape(1, -1)
    y_prompt = _trunk(x_prompt, p0, f0, p1, f1, g0)
    y_sample = _trunk(x_sample, p0, f0, p1, f1, g0)
    return (y_prompt, y_sample)
```

```python
import functools
import math

import numpy as np
import jax
import jax.numpy as jnp
from jax import lax
from jax.experimental import pallas as pl
from jax.experimental.pallas import tpu as pltpu

F32 = jnp.float32
BF16 = jnp.bfloat16

D_MODEL = 1024
GRID_W = 64
HEAD_DIM = 64
EPS = 1e-6

NA_HEADS = 8
NA_WIN_R = 8
NA_WIN_C = 16
NA_W = NA_HEADS * HEAD_DIM
NA_ROWS_PER_STEP = 8
NA_KEY_ROWS = 16
NA_KEY_BLOCK_ROWS = 4

DIFF_HEADS = 4
DIFF_QK_DIM = 64
DIFF_V_DIM = 128
DIFF_W = 512

MLA_HEADS = 16
MLA_Q_RANK = 384
MLA_KV_RANK = 256
MLA_NOPE = 64
MLA_ROPE = 32
MLA_V = 64
MLA_QK = MLA_NOPE + MLA_ROPE
MLA_QK_PAD = 128
ROPE_THETA = 10000.0

D_FF = 2816
FF_CHUNK = 256

VMEM_LIMIT = 56 * 1024 * 1024

TOKEN_TILE = 512
ATT_TQ = 512
ATT_TKC = 512
ONES_PAD = 16
DIFF_VT = DIFF_V_DIM + ONES_PAD
MLA_VT = MLA_V + ONES_PAD
LOG2E = math.log2(math.e)

_NT = (((1,), (1,)), ((), ()))
_TN = (((0,), (0,)), ((), ()))


def _rms(x, g):
    ms = jnp.mean(x * x, axis=-1, keepdims=True)
    return (x * lax.rsqrt(ms + EPS)) * g


def _const_spec(shape):
    nd = len(shape)
    return pl.BlockSpec(shape, lambda *_: (0,) * nd, pipeline_mode=pl.Buffered(1))


def _params(n_axes):
    return pltpu.CompilerParams(
        dimension_semantics=("parallel",) * n_axes, vmem_limit_bytes=VMEM_LIMIT)


def _with_ones_rows(vt, per_head, data_rows):
    row = lax.broadcasted_iota(jnp.int32, vt.shape, 0)
    return jnp.where(row % per_head == data_rows, 1.0, vt)


def _proj0_kernel(x_ref, g_ref, w_ref, wt_ref,
                  naq_ref, nak_ref, nav_ref, dk_ref, dqt_ref, dvt_ref):
    h = _rms(x_ref[...], g_ref[...]).astype(BF16)
    for i, o_ref in enumerate((naq_ref, nak_ref, nav_ref, dk_ref)):
        o_ref[...] = jnp.dot(h, w_ref[:, i * 512:(i + 1) * 512],
                             preferred_element_type=F32).astype(BF16)
    t = lax.dot_general(wt_ref[...], h, _NT, preferred_element_type=F32)
    dqt_ref[...] = (t[:DIFF_W] * LOG2E).astype(BF16)
    dvt_ref[...] = _with_ones_rows(t[DIFF_W:], DIFF_VT, DIFF_V_DIM).astype(BF16)


def _proj0(x, g, w, wt, B, S):
    N = B * S
    tn = TOKEN_TILE
    spb = S // tn
    tok = pl.BlockSpec((tn, 512), lambda i: (i, 0))
    nvt = DIFF_HEADS * DIFF_VT
    return pl.pallas_call(
        _proj0_kernel,
        grid=(N // tn,),
        in_specs=[pl.BlockSpec((tn, D_MODEL), lambda i: (i, 0)),
                  _const_spec((1, D_MODEL)),
                  _const_spec(w.shape),
                  _const_spec(wt.shape)],
        out_specs=[tok, tok, tok, tok,
                   pl.BlockSpec((None, DIFF_W, tn), lambda i: (i // spb, 0, i % spb)),
                   pl.BlockSpec((None, nvt, tn), lambda i: (i // spb, 0, i % spb))],
        out_shape=[jax.ShapeDtypeStruct((N, 512), BF16)] * 4
        + [jax.ShapeDtypeStruct((B, DIFF_W, S), BF16),
           jax.ShapeDtypeStruct((B, nvt, S), BF16)],
        compiler_params=_params(1),
        name="proj0",
    )(x, g, w, wt)


def _na_kernel(q_ref, k0, k1, k2, k3, v0, v1, v2, v3, bias_ref, o_ref,
               kbuf, vbuf, s_scr, *, rows):
    rb = pl.program_id(1)
    blk = NA_KEY_BLOCK_ROWS * GRID_W
    for i, (kr, vr) in enumerate(((k0, v0), (k1, v1), (k2, v2), (k3, v3))):
        kbuf[i * blk:(i + 1) * blk, :] = kr[...]
        vbuf[i * blk:(i + 1) * blk, :] = vr[...]
    kw0 = jnp.clip(rb * NA_ROWS_PER_STEP - NA_WIN_R // 2, 0, rows - NA_KEY_ROWS)
    nkeys = NA_WIN_R * GRID_W

    def window(rl):
        r = rb * NA_ROWS_PER_STEP + rl
        rs = jnp.clip(r - NA_WIN_R // 2, 0, rows - NA_WIN_R)
        off = pl.multiple_of((rs - kw0) * GRID_W, GRID_W)
        return off, rs - r + NA_WIN_R - 1

    def qk(slot, rl):
        off, dr0 = window(rl)
        qrow = q_ref[pl.ds(pl.multiple_of(rl * GRID_W, GRID_W), GRID_W), :]
        kwin = kbuf[pl.ds(off, nkeys), :]
        for h in range(NA_HEADS):
            sl = slice(h * HEAD_DIM, (h + 1) * HEAD_DIM)
            s = lax.dot_general(qrow[:, sl], kwin[:, sl], _NT, preferred_element_type=F32)
            s_scr[slot, h] = s + bias_ref[dr0, h]

    def sm_pv(slot, rl, carry):
        off, _ = window(rl)
        vwin = vbuf[pl.ds(off, nkeys), :]
        outs = []
        for h in range(NA_HEADS):
            sl = slice(h * HEAD_DIM, (h + 1) * HEAD_DIM)
            s = s_scr[slot, h]
            m = jnp.max(s, axis=-1, keepdims=True)
            p = jnp.exp(s - m)
            l = jnp.sum(p, axis=-1, keepdims=True)
            o = jnp.dot(p.astype(BF16), vwin[:, sl], preferred_element_type=F32)
            outs.append(o * (1.0 / l))
        qoff = pl.multiple_of(rl * GRID_W, GRID_W)
        o_ref[pl.ds(qoff, GRID_W), :] = jnp.concatenate(outs, axis=1).astype(BF16)
        return carry

    _flash_pair(NA_ROWS_PER_STEP, qk, sm_pv, 0)


def _na_attn(q, k, v, bias, B, S):
    rows = S // GRID_W
    assert rows % NA_ROWS_PER_STEP == 0 and rows >= NA_KEY_ROWS
    N = B * S
    tq = NA_ROWS_PER_STEP * GRID_W
    blk = NA_KEY_BLOCK_ROWS * GRID_W
    bpb = rows // NA_KEY_BLOCK_ROWS

    def kv_spec(i):
        def idx(b, rb):
            kw0 = jnp.clip(rb * NA_ROWS_PER_STEP - NA_WIN_R // 2, 0, rows - NA_KEY_ROWS)
            return (b * bpb + kw0 // NA_KEY_BLOCK_ROWS + i, 0)
        return pl.BlockSpec((blk, NA_W), idx)

    kv_specs = [kv_spec(i) for i in range(NA_KEY_ROWS // NA_KEY_BLOCK_ROWS)]
    qspec = pl.BlockSpec((tq, NA_W), lambda b, rb: (b * (rows // NA_ROWS_PER_STEP) + rb, 0))
    return pl.pallas_call(
        functools.partial(_na_kernel, rows=rows),
        grid=(B, rows // NA_ROWS_PER_STEP),
        in_specs=[qspec] + kv_specs + kv_specs + [_const_spec(bias.shape)],
        out_specs=qspec,
        out_shape=jax.ShapeDtypeStruct((N, NA_W), BF16),
        scratch_shapes=[pltpu.VMEM((NA_KEY_ROWS * GRID_W, NA_W), BF16),
                        pltpu.VMEM((NA_KEY_ROWS * GRID_W, NA_W), BF16),
                        pltpu.VMEM((2, NA_HEADS, GRID_W, NA_WIN_R * GRID_W), F32)],
        compiler_params=_params(2),
        name="na_attn",
    )(q, k, k, k, k, v, v, v, v, bias)


def _na_bias_table(rpb):
    c = np.arange(GRID_W)[:, None]
    cp = np.arange(GRID_W)[None, :]
    win0 = np.clip(c - NA_WIN_C // 2, 0, GRID_W - NA_WIN_C)
    ok = (cp >= win0) & (cp < win0 + NA_WIN_C)
    cidx = np.clip(cp - c + NA_WIN_C - 1, 0, 2 * NA_WIN_C - 2)
    t = rpb.astype(F32)[:, :, cidx]
    t = jnp.where(jnp.asarray(ok)[None, None], t, -jnp.inf)
    per = [t[:, d:d + NA_WIN_R].transpose(0, 2, 1, 3).reshape(NA_HEADS, GRID_W, NA_WIN_R * GRID_W)
           for d in range(NA_WIN_R)]
    return jnp.stack(per, axis=0)


def _flash_pair(n_chunks, qk, sm_pv, init):
    def body(i, carry):
        c = 2 * i
        qk(1, c + 1)
        carry = sm_pv(0, c, carry)
        qk(0, c + 2)
        return sm_pv(1, c + 1, carry)

    qk(0, 0)
    carry = lax.fori_loop(0, n_chunks // 2 - 1, body, init)
    qk(1, n_chunks - 1)
    carry = sm_pv(0, n_chunks - 2, carry)
    return sm_pv(1, n_chunks - 1, carry)


def _online_softmax_step(s, m_prev, a_prev, vt_chunk, off=None):
    m_chunk = jnp.max(s, axis=0, keepdims=True)
    if off is not None:
        m_chunk = m_chunk - off
    m_new = jnp.maximum(m_prev, m_chunk)
    alpha = jnp.exp2(m_prev - m_new)
    p = jnp.exp2(s - (m_new if off is None else m_new + off)).astype(BF16)
    a_new = alpha * a_prev + jnp.dot(vt_chunk, p, preferred_element_type=F32)
    return m_new, a_new


def _diff_kernel(slopes_ref, qt_ref, k_ref, vt_ref, lamv_ref, gsub_ref, o_ref, s_scr, b_scr,
                 *, S, tq, tkc, hp, lam_init):
    g = pl.program_id(1)
    qi = pl.program_id(2)
    row = lax.broadcasted_iota(jnp.int32, (tkc, tq), 0)
    col = lax.broadcasted_iota(jnp.int32, (tkc, tq), 1)
    u = (col - row).astype(F32)
    lv = lamv_ref[...]
    lam = (jnp.exp(jnp.sum(lv[0:1] * lv[1:2], axis=-1, keepdims=True))
           - jnp.exp(jnp.sum(lv[2:3] * lv[3:4], axis=-1, keepdims=True)) + lam_init)
    zeros = jnp.zeros((DIFF_QK_DIM, tq), BF16)

    for hh in range(hp):
        slope = slopes_ref[g * hp + hh]
        ksl = slice(hh * 128, (hh + 1) * 128)
        qpair = qt_ref[ksl, :]
        qts = (jnp.concatenate([qpair[:DIFF_QK_DIM], zeros], axis=0),
               jnp.concatenate([zeros, qpair[DIFF_QK_DIM:]], axis=0))

        def qk(slot, c):
            k0 = pl.multiple_of(c * tkc, tkc)
            kc = k_ref[pl.ds(k0, tkc), ksl]
            for j in range(2):
                s_scr[slot, j] = jnp.dot(kc, qts[j], preferred_element_type=F32)

        b_scr[0] = slope * u
        b_scr[1] = -slope * u
        b_scr[2] = slope * jnp.abs(u)

        def sm_pv(slot, c, carry):
            k0 = pl.multiple_of(c * tkc, tkc)
            dc = qi - jnp.asarray(c, jnp.int32)
            side = jnp.where(dc > 0, 0, jnp.where(dc < 0, 1, 2))
            off = slope * (jnp.abs(dc) * tq).astype(F32)
            vc = vt_ref[hh * DIFF_VT:(hh + 1) * DIFF_VT, pl.ds(k0, tkc)]
            out = []
            for j in range(2):
                m_new, a_new = _online_softmax_step(
                    s_scr[slot, j] - b_scr[side], carry[2 * j], carry[2 * j + 1], vc, off)
                out += [m_new, a_new]
            return tuple(out)

        init = (jnp.full((1, tq), -jnp.inf, F32), jnp.zeros((DIFF_VT, tq), F32)) * 2
        _, a0, _, a1 = _flash_pair(S // tkc, qk, sm_pv, init)
        o0 = a0[:DIFF_V_DIM] * (1.0 / a0[DIFF_V_DIM:DIFF_V_DIM + 1])
        o1 = a1[:DIFF_V_DIM] * (1.0 / a1[DIFF_V_DIM:DIFF_V_DIM + 1])
        o = o0 - lam * o1
        ms = jnp.mean(o * o, axis=0, keepdims=True)
        y = (o * lax.rsqrt(ms + EPS)) * gsub_ref[...]
        o_ref[ksl, :] = (y * (1.0 - lam_init)).astype(BF16)


def _diff_attn(qt, k, vt, lamv, gsub, B, S, lam_init):
    tq = tkc = ATT_TQ
    hp = 2 if S <= 4096 else 1
    assert S % tq == 0 and (S // tkc) % 2 == 0 and DIFF_HEADS % hp == 0
    nq = S // tq
    slopes = jnp.asarray(
        2.0 ** (-8.0 * np.arange(1, DIFF_HEADS + 1) / DIFF_HEADS) * LOG2E, F32)
    return pl.pallas_call(
        functools.partial(_diff_kernel, S=S, tq=tq, tkc=tkc, hp=hp, lam_init=lam_init),
        grid=(B, DIFF_HEADS // hp, nq),
        in_specs=[pl.BlockSpec(memory_space=pltpu.SMEM),
                  pl.BlockSpec((None, hp * 128, tq), lambda b, g, i: (b, g, i)),
                  pl.BlockSpec((S, hp * 128), lambda b, g, i: (b, g)),
                  pl.BlockSpec((None, hp * DIFF_VT, S), lambda b, g, i: (b, g, 0)),
                  _const_spec((4, DIFF_QK_DIM)),
                  _const_spec((DIFF_V_DIM, 1))],
        out_specs=pl.BlockSpec((None, hp * 128, tq), lambda b, g, i: (b, g, i)),
        out_shape=jax.ShapeDtypeStruct((B, DIFF_W, S), BF16),
        scratch_shapes=[pltpu.VMEM((2, 2, tkc, tq), F32), pltpu.VMEM((3, tkc, tq), F32)],
        compiler_params=_params(3),
        name="diff_attn",
    )(slopes, qt, k, vt, lamv, gsub)


def _ffn_kernel(*refs, n_tok):
    a_refs = refs[:n_tok + 1]
    (x_ref, wo_ref, gpm_ref, gpf_ref, wg_ref, wu_ref, wout_ref, gpost_ref,
     y_ref) = refs[n_tok + 1:]
    m = None
    r0 = 0
    for a_ref in a_refs[:n_tok]:
        w = a_ref.shape[1]
        t = jnp.dot(a_ref[...], wo_ref[r0:r0 + w, :], preferred_element_type=F32)
        m = t if m is None else m + t
        r0 += w
    at_ref = a_refs[n_tok]
    t = lax.dot_general(at_ref[...], wo_ref[r0:r0 + at_ref.shape[0], :], _TN,
                        preferred_element_type=F32)
    m = t if m is None else m + t
    x1 = x_ref[...] + _rms(m, gpm_ref[...])
    h = _rms(x1, gpf_ref[...]).astype(BF16)
    acc = jnp.zeros(x1.shape, F32)
    for c in range(D_FF // FF_CHUNK):
        sl = slice(c * FF_CHUNK, (c + 1) * FF_CHUNK)
        g = jnp.dot(h, wg_ref[:, sl], preferred_element_type=F32)
        u = jnp.dot(h, wu_ref[:, sl], preferred_element_type=F32)
        act = (g * (1.0 / (1.0 + jnp.exp(-g)))) * u
        acc = acc + jnp.dot(act.astype(BF16), wout_ref[sl, :], preferred_element_type=F32)
    y_ref[...] = x1 + _rms(acc, gpost_ref[...])


def _out_ffn(a_tok, a_t, x, wo, gpm, gpf, wg, wu, wout, gpost, S):
    N = x.shape[0]
    tn = TOKEN_TILE
    spb = S // tn
    wt = a_t.shape[1]
    return pl.pallas_call(
        functools.partial(_ffn_kernel, n_tok=len(a_tok)),
        grid=(N // tn,),
        in_specs=[pl.BlockSpec((tn, a.shape[1]), lambda i: (i, 0)) for a in a_tok]
        + [pl.BlockSpec((None, wt, tn), lambda i: (i // spb, 0, i % spb)),
           pl.BlockSpec((tn, D_MODEL), lambda i: (i, 0)),
           _const_spec((D_MODEL, D_MODEL)),
           _const_spec((1, D_MODEL)), _const_spec((1, D_MODEL)),
           _const_spec((D_MODEL, D_FF)), _const_spec((D_MODEL, D_FF)),
           _const_spec((D_FF, D_MODEL)), _const_spec((1, D_MODEL))],
        out_specs=pl.BlockSpec((tn, D_MODEL), lambda i: (i, 0)),
        out_shape=jax.ShapeDtypeStruct((N, D_MODEL), F32),
        compiler_params=_params(1),
        name="out_ffn",
    )(*a_tok, a_t, x, wo, gpm, gpf, wg, wu, wout, gpost)


MLA_Q_GROUP = 4


def _mla_proj_kernel(x_ref, g_ref, wd_ref, gq_ref, gkv_ref, wqt_ref, wqrt_ref,
                     wkn_ref, wvt_ref, ctt_ref, stt_ref, ck_ref, sk_ref,
                     qt_ref, k_ref, vt_ref):
    h = _rms(x_ref[...], g_ref[...]).astype(BF16)
    c = jnp.dot(h, wd_ref[...], preferred_element_type=F32)
    cqn = _rms(c[:, :MLA_Q_RANK], gq_ref[...]).astype(BF16)
    o1 = MLA_Q_RANK + MLA_KV_RANK
    ckvn = _rms(c[:, MLA_Q_RANK:o1], gkv_ref[...]).astype(BF16)
    krope = c[:, o1:o1 + MLA_QK_PAD] * ck_ref[...] + c[:, o1 + MLA_QK_PAD:] * sk_ref[...]

    gw = MLA_Q_GROUP * MLA_QK_PAD
    ctt = jnp.concatenate([ctt_ref[...]] * MLA_Q_GROUP, axis=0)
    stt = jnp.concatenate([stt_ref[...]] * MLA_Q_GROUP, axis=0)
    for p in range(MLA_HEADS // MLA_Q_GROUP):
        sl = slice(p * gw, (p + 1) * gw)
        a = lax.dot_general(wqt_ref[sl, :], cqn, _NT, preferred_element_type=F32)
        b = lax.dot_general(wqrt_ref[sl, :], cqn, _NT, preferred_element_type=F32)
        qt_ref[sl, :] = (a * ctt + b * stt).astype(BF16)

    for p in range(MLA_HEADS // MLA_Q_GROUP):
        sl = slice(p * gw, (p + 1) * gw)
        kn = jnp.dot(ckvn, wkn_ref[:, sl], preferred_element_type=F32)
        k_ref[:, sl] = (kn + jnp.concatenate([krope] * MLA_Q_GROUP, axis=1)).astype(BF16)

    vt = lax.dot_general(wvt_ref[...], ckvn, _NT, preferred_element_type=F32)
    vt_ref[...] = _with_ones_rows(vt, MLA_VT, MLA_V).astype(BF16)


def _mla_proj(x, p1, tables, B, S):
    N = B * S
    tn = TOKEN_TILE
    spb = S // tn
    qw = MLA_HEADS * MLA_QK_PAD
    nvt = MLA_HEADS * MLA_VT
    ctt, stt, ck, sk = tables
    return pl.pallas_call(
        _mla_proj_kernel,
        grid=(N // tn,),
        in_specs=[pl.BlockSpec((tn, D_MODEL), lambda i: (i, 0)),
                  _const_spec((1, D_MODEL)),
                  _const_spec(p1['wd'].shape), _const_spec((1, MLA_Q_RANK)),
                  _const_spec((1, MLA_KV_RANK)),
                  _const_spec(p1['wqt'].shape), _const_spec(p1['wqrt'].shape),
                  _const_spec(p1['wkn'].shape), _const_spec(p1['wvt'].shape),
                  pl.BlockSpec((MLA_QK_PAD, tn), lambda i: (0, i % spb)),
                  pl.BlockSpec((MLA_QK_PAD, tn), lambda i: (0, i % spb)),
                  pl.BlockSpec((tn, MLA_QK_PAD), lambda i: (i % spb, 0)),
                  pl.BlockSpec((tn, MLA_QK_PAD), lambda i: (i % spb, 0))],
        out_specs=[pl.BlockSpec((None, qw, tn), lambda i: (i // spb, 0, i % spb)),
                   pl.BlockSpec((tn, qw), lambda i: (i, 0)),
                   pl.BlockSpec((None, nvt, tn), lambda i: (i // spb, 0, i % spb))],
        out_shape=[jax.ShapeDtypeStruct((B, qw, S), BF16),
                   jax.ShapeDtypeStruct((N, qw), BF16),
                   jax.ShapeDtypeStruct((B, nvt, S), BF16)],
        compiler_params=_params(1),
        name="mla_proj",
    )(x, p1['g'], p1['wd'], p1['gq'], p1['gkv'], p1['wqt'], p1['wqrt'],
      p1['wkn'], p1['wvt'], ctt, stt, ck, sk)


def _mla_kernel(qt_ref, k_ref, vt_ref, o_ref, s_scr, *, S, tq, tkc, hp):
    for h0 in range(0, hp, 2):
        heads = (h0, h0 + 1)
        qts = [qt_ref[h * MLA_QK_PAD:(h + 1) * MLA_QK_PAD, :] for h in heads]

        def qk(slot, c):
            k0 = pl.multiple_of(c * tkc, tkc)
            for j, h in enumerate(heads):
                kc = k_ref[pl.ds(k0, tkc), h * MLA_QK_PAD:(h + 1) * MLA_QK_PAD]
                s_scr[slot, j] = jnp.dot(kc, qts[j], preferred_element_type=F32)

        def sm_pv(slot, c, carry):
            k0 = pl.multiple_of(c * tkc, tkc)
            out = []
            for j, h in enumerate(heads):
                vc = vt_ref[h * MLA_VT:(h + 1) * MLA_VT, pl.ds(k0, tkc)]
                m_new, a_new = _online_softmax_step(
                    s_scr[slot, j], carry[2 * j], carry[2 * j + 1], vc)
                out += [m_new, a_new]
            return tuple(out)

        init = (jnp.full((1, tq), -jnp.inf, F32), jnp.zeros((MLA_VT, tq), F32)) * 2
        res = _flash_pair(S // tkc, qk, sm_pv, init)
        for j, h in enumerate(heads):
            a = res[2 * j + 1]
            o_ref[h * MLA_V:(h + 1) * MLA_V, :] = (
                a[:MLA_V] * (1.0 / a[MLA_V:MLA_V + 1])).astype(BF16)


def _mla_attn(qt, k, vt, B, S):
    tq, tkc = ATT_TQ, ATT_TKC
    hp = 4 if S <= 4096 else 2
    assert S % tq == 0 and (S // tkc) % 2 == 0 and MLA_HEADS % hp == 0
    nq = S // tq
    return pl.pallas_call(
        functools.partial(_mla_kernel, S=S, tq=tq, tkc=tkc, hp=hp),
        grid=(B, MLA_HEADS // hp, nq),
        in_specs=[pl.BlockSpec((None, hp * MLA_QK_PAD, tq), lambda b, g, i: (b, g, i)),
                  pl.BlockSpec((S, hp * MLA_QK_PAD), lambda b, g, i: (b, g)),
                  pl.BlockSpec((None, hp * MLA_VT, S), lambda b, g, i: (b, g, 0))],
        out_specs=pl.BlockSpec((None, hp * MLA_V, tq), lambda b, g, i: (b, g, i)),
        out_shape=jax.ShapeDtypeStruct((B, MLA_HEADS * MLA_V, S), BF16),
        scratch_shapes=[pltpu.VMEM((2, 2, tkc, tq), F32)],
        compiler_params=_params(3),
        name="mla_attn",
    )(qt, k, vt)


def _rot_half_cols(w):
    half = MLA_ROPE // 2
    return jnp.concatenate([-w[..., half:], w[..., :half]], axis=-1)


def _prep_layer0(p):
    w = p['w_in']
    qscale = HEAD_DIM ** -0.5
    o0 = 3 * NA_W
    w_main = jnp.concatenate([w[:, 0:NA_W] * qscale, w[:, NA_W:2 * NA_W], w[:, 2 * NA_W:o0],
                              w[:, o0 + DIFF_W:o0 + 2 * DIFF_W]], axis=1).astype(BF16)
    wq_t = (w[:, o0:o0 + DIFF_W] * (DIFF_QK_DIM ** -0.5)).T
    wv3 = w[:, o0 + 2 * DIFF_W:].T.reshape(DIFF_HEADS, DIFF_V_DIM, D_MODEL)
    wv_t = jnp.concatenate([wv3, jnp.zeros((DIFF_HEADS, ONES_PAD, D_MODEL), F32)],
                           axis=1).reshape(DIFF_HEADS * DIFF_VT, D_MODEL)
    wt = jnp.concatenate([wq_t, wv_t], axis=0).astype(BF16)
    lamv = jnp.stack([p['lam_q1'], p['lam_k1'], p['lam_q2'], p['lam_k2']]).astype(F32)
    return dict(w_main=w_main, wt=wt, lamv=lamv,
                bias=_na_bias_table(p['rpb']),
                gsub=p['g_sub'].reshape(-1, 1).astype(F32))


def _prep_ffn(p):
    w = p['w_ffn_in']
    return dict(wo=p['w_out'].astype(BF16),
                gpm=p['g_post_mix'].reshape(1, -1), gpf=p['g_pre_ffn'].reshape(1, -1),
                wg=w[:, :D_FF].astype(BF16), wu=w[:, D_FF:].astype(BF16),
                wout=p['w_ffn_out'].astype(BF16), gpost=p['g_post_ffn'].reshape(1, -1))


def _prep_layer1(p):
    wd = p['w_down']
    o1 = MLA_Q_RANK + MLA_KV_RANK
    kr = wd[:, o1:]
    zl = jnp.zeros((D_MODEL, MLA_NOPE), F32)
    zr = jnp.zeros((D_MODEL, MLA_QK_PAD - MLA_QK), F32)
    wd_all = jnp.concatenate([wd[:, :o1], zl, kr, zr, zl, _rot_half_cols(kr), zr],
                             axis=1).astype(BF16)
    wq3 = p['w_uq'].reshape(MLA_Q_RANK, MLA_HEADS, MLA_QK)
    zq = jnp.zeros((MLA_Q_RANK, MLA_HEADS, MLA_QK_PAD - MLA_QK), F32)
    wq = jnp.concatenate([wq3, zq], axis=-1).reshape(MLA_Q_RANK, -1)
    zn = jnp.zeros((MLA_Q_RANK, MLA_HEADS, MLA_NOPE), F32)
    wqr = jnp.concatenate([zn, _rot_half_cols(wq3[..., MLA_NOPE:]), zq],
                          axis=-1).reshape(MLA_Q_RANK, -1)
    wkv3 = p['w_ukv'].reshape(MLA_KV_RANK, MLA_HEADS, MLA_NOPE + MLA_V)
    zk = jnp.zeros((MLA_KV_RANK, MLA_HEADS, MLA_QK_PAD - MLA_NOPE), F32)
    wkn = jnp.concatenate([wkv3[..., :MLA_NOPE], zk], axis=-1).reshape(MLA_KV_RANK, -1)
    wv3 = wkv3[..., MLA_NOPE:].transpose(1, 2, 0)
    wvt = jnp.concatenate([wv3, jnp.zeros((MLA_HEADS, ONES_PAD, MLA_KV_RANK), F32)],
                          axis=1).reshape(MLA_HEADS * MLA_VT, MLA_KV_RANK)
    return dict(wd=wd_all, wqt=wq.T.astype(BF16), wqrt=wqr.T.astype(BF16),
                wkn=wkn.astype(BF16), wvt=wvt.astype(BF16),
                g=p['g_pre_mix'].reshape(1, -1), gq=p['g_q'].reshape(1, -1),
                gkv=p['g_kv'].reshape(1, -1))


def _rope_tables(S):
    half = MLA_ROPE // 2
    inv = jnp.asarray(ROPE_THETA ** (-np.arange(half, dtype=np.float32) * 2.0 / MLA_ROPE), F32)
    ang = jnp.arange(S).astype(F32)[:, None] * inv[None, :]
    cos2 = jnp.concatenate([jnp.cos(ang)] * 2, axis=1)
    sin2 = jnp.concatenate([jnp.sin(ang)] * 2, axis=1)
    ones = jnp.ones((S, MLA_NOPE), F32)
    z_nope = jnp.zeros((S, MLA_NOPE), F32)
    z_pad = jnp.zeros((S, MLA_QK_PAD - MLA_QK), F32)
    ck = jnp.concatenate([z_nope, cos2, z_pad], axis=1)
    sk = jnp.concatenate([z_nope, sin2, z_pad], axis=1)
    qscale = MLA_QK ** -0.5 * LOG2E
    ctt = (jnp.concatenate([ones, cos2, z_pad], axis=1) * qscale).T
    stt = (sk * qscale).T
    return ctt, stt, ck, sk


def _trunk(x3, p0, f0, p1, f1, g0):
    B, S, _ = x3.shape
    x = x3.reshape(B * S, D_MODEL)
    lam_init0 = 0.8 - 0.6 * math.exp(-0.3 * 0)
    naq, nak, nav, dk, dqt, dvt = _proj0(x, g0, p0['w_main'], p0['wt'], B, S)
    a = _na_attn(naq, nak, nav, p0['bias'], B, S)
    bt = _diff_attn(dqt, dk, dvt, p0['lamv'], p0['gsub'], B, S, lam_init0)
    x = _out_ffn([a], bt, x, f0['wo'], f0['gpm'], f0['gpf'], f0['wg'], f0['wu'],
                 f0['wout'], f0['gpost'], S)
    qt, k, vt = _mla_proj(x, p1, _rope_tables(S), B, S)
    ot = _mla_attn(qt, k, vt, B, S)
    x = _out_ffn([], ot, x, f1['wo'], f1['gpm'], f1['gpf'], f1['wg'], f1['wu'],
                 f1['wout'], f1['gpost'], S)
    return x.reshape(B, S, D_MODEL)


def kernel(x_prompt, x_sample, l0_g_pre_mix, l0_w_in, l0_rpb, l0_lam_q1, l0_lam_k1, l0_lam_q2, l0_lam_k2, l0_g_sub, l0_w_out, l0_g_post_mix, l0_g_pre_ffn, l0_w_ffn_in, l0_w_ffn_out, l0_g_post_ffn, l1_g_pre_mix, l1_w_down, l1_g_q, l1_g_kv, l1_w_uq, l1_w_ukv, l1_w_out, l1_g_post_mix, l1_g_pre_ffn, l1_w_ffn_in, l1_w_ffn_out, l1_g_post_ffn):
    p0 = _prep_layer0(dict(w_in=l0_w_in, rpb=l0_rpb, lam_q1=l0_lam_q1, lam_k1=l0_lam_k1,
                           lam_q2=l0_lam_q2, lam_k2=l0_lam_k2, g_sub=l0_g_sub))
    f0 = _prep_ffn(dict(w_out=l0_w_out, g_post_mix=l0_g_post_mix, g_pre_ffn=l0_g_pre_ffn,
                        w_ffn_in=l0_w_ffn_in, w_ffn_out=l0_w_ffn_out, g_post_ffn=l0_g_post_ffn))
    p1 = _prep_layer1(dict(g_pre_mix=l1_g_pre_mix, w_down=l1_w_down, g_q=l1_g_q, g_kv=l1_g_kv,
                           w_uq=l1_w_uq, w_ukv=l1_w_ukv))
    f1 = _prep_ffn(dict(w_out=l1_w_out, g_post_mix=l1_g_post_mix, g_pre_ffn=l1_g_pre_ffn,
                        w_ffn_in=l1_w_ffn_in, w_ffn_out=l1_w_ffn_out, g_post_ffn=l1_g_post_ffn))
    g0 = l0_g_pre_mix.reshape(1, -1)
    y_prompt = _trunk(x_prompt, p0, f0, p1, f1, g0)
    y_sample = _trunk(x_sample, p0, f0, p1, f1, g0)
    return (y_prompt, y_sample)
```

```python
import functools
import math

import numpy as np
import jax
import jax.numpy as jnp
from jax import lax
from jax.experimental import pallas as pl
from jax.experimental.pallas import tpu as pltpu

F32 = jnp.float32
BF16 = jnp.bfloat16

D_MODEL = 1024
GRID_W = 64
HEAD_DIM = 64
EPS = 1e-6

NA_HEADS = 8
NA_WIN_R = 8
NA_WIN_C = 16
NA_W = NA_HEADS * HEAD_DIM
NA_ROWS_PER_STEP = 8
NA_KEY_ROWS = 16
NA_KEY_BLOCK_ROWS = 4

DIFF_HEADS = 4
DIFF_QK_DIM = 64
DIFF_V_DIM = 128
DIFF_W = 512

MLA_HEADS = 16
MLA_Q_RANK = 384
MLA_KV_RANK = 256
MLA_NOPE = 64
MLA_ROPE = 32
MLA_V = 64
MLA_QK = MLA_NOPE + MLA_ROPE
MLA_QK_PAD = 128
ROPE_THETA = 10000.0

D_FF = 2816
FF_CHUNK = 256

VMEM_LIMIT = 56 * 1024 * 1024

TOKEN_TILE = 512
ATT_TQ = 512
ATT_TKC = 512
ONES_PAD = 16
DIFF_VT = DIFF_V_DIM + ONES_PAD
MLA_VT = MLA_V + ONES_PAD
LOG2E = math.log2(math.e)

_NT = (((1,), (1,)), ((), ()))
_TN = (((0,), (0,)), ((), ()))


def _rms(x, g):
    ms = jnp.mean(x * x, axis=-1, keepdims=True)
    return (x * lax.rsqrt(ms + EPS)) * g


def _const_spec(shape):
    nd = len(shape)
    return pl.BlockSpec(shape, lambda *_: (0,) * nd, pipeline_mode=pl.Buffered(1))


def _params(n_axes):
    return pltpu.CompilerParams(
        dimension_semantics=("parallel",) * n_axes, vmem_limit_bytes=VMEM_LIMIT)


def _with_ones_rows(vt, per_head, data_rows):
    row = lax.broadcasted_iota(jnp.int32, vt.shape, 0)
    return jnp.where(row % per_head == data_rows, 1.0, vt)


def _proj0_kernel(x_ref, g_ref, w_ref, wt_ref,
                  naq_ref, nak_ref, nav_ref, dk_ref, dqt_ref, dvt_ref):
    h = _rms(x_ref[...], g_ref[...]).astype(BF16)
    for i, o_ref in enumerate((naq_ref, nak_ref, nav_ref, dk_ref)):
        o_ref[...] = jnp.dot(h, w_ref[:, i * 512:(i + 1) * 512],
                             preferred_element_type=F32).astype(BF16)
    t = lax.dot_general(wt_ref[...], h, _NT, preferred_element_type=F32)
    dqt_ref[...] = (t[:DIFF_W] * LOG2E).astype(BF16)
    dvt_ref[...] = _with_ones_rows(t[DIFF_W:], DIFF_VT, DIFF_V_DIM).astype(BF16)


def _proj0(x, g, w, wt, B, S):
    N = B * S
    tn = TOKEN_TILE
    spb = S // tn
    tok = pl.BlockSpec((tn, 512), lambda i: (i, 0))
    nvt = DIFF_HEADS * DIFF_VT
    return pl.pallas_call(
        _proj0_kernel,
        grid=(N // tn,),
        in_specs=[pl.BlockSpec((tn, D_MODEL), lambda i: (i, 0)),
                  _const_spec((1, D_MODEL)),
                  _const_spec(w.shape),
                  _const_spec(wt.shape)],
        out_specs=[tok, tok, tok, tok,
                   pl.BlockSpec((None, DIFF_W, tn), lambda i: (i // spb, 0, i % spb)),
                   pl.BlockSpec((None, nvt, tn), lambda i: (i // spb, 0, i % spb))],
        out_shape=[jax.ShapeDtypeStruct((N, 512), BF16)] * 4
        + [jax.ShapeDtypeStruct((B, DIFF_W, S), BF16),
           jax.ShapeDtypeStruct((B, nvt, S), BF16)],
        compiler_params=_params(1),
        name="proj0",
    )(x, g, w, wt)


def _na_kernel(q_ref, k0, k1, k2, k3, v0, v1, v2, v3, bias_ref, o_ref,
               kbuf, vbuf, s_scr, *, rows):
    rb = pl.program_id(1)
    blk = NA_KEY_BLOCK_ROWS * GRID_W
    for i, (kr, vr) in enumerate(((k0, v0), (k1, v1), (k2, v2), (k3, v3))):
        kblk = kr[...]
        vblk = vr[...]
        for h in range(NA_HEADS):
            sl = slice(h * HEAD_DIM, (h + 1) * HEAD_DIM)
            kbuf[h, i * blk:(i + 1) * blk, :] = kblk[:, sl]
            vbuf[h, i * blk:(i + 1) * blk, :] = vblk[:, sl]
    kw0 = jnp.clip(rb * NA_ROWS_PER_STEP - NA_WIN_R // 2, 0, rows - NA_KEY_ROWS)
    nkeys = NA_WIN_R * GRID_W

    def window(rl):
        r = rb * NA_ROWS_PER_STEP + rl
        rs = jnp.clip(r - NA_WIN_R // 2, 0, rows - NA_WIN_R)
        off = pl.multiple_of((rs - kw0) * GRID_W, GRID_W)
        return off, rs - r + NA_WIN_R - 1

    def qk(slot, rl):
        off, dr0 = window(rl)
        qrow = q_ref[pl.ds(pl.multiple_of(rl * GRID_W, GRID_W), GRID_W), :]
        for h in range(NA_HEADS):
            sl = slice(h * HEAD_DIM, (h + 1) * HEAD_DIM)
            s = lax.dot_general(qrow[:, sl], kbuf[h, pl.ds(off, nkeys), :], _NT,
                                preferred_element_type=F32)
            s_scr[slot, h] = s + bias_ref[dr0, h]

    def sm_pv(slot, rl, carry):
        off, _ = window(rl)
        outs = []
        for h in range(NA_HEADS):
            s = s_scr[slot, h]
            m = jnp.max(s, axis=-1, keepdims=True)
            p = jnp.exp(s - m)
            l = jnp.sum(p, axis=-1, keepdims=True)
            o = jnp.dot(p.astype(BF16), vbuf[h, pl.ds(off, nkeys), :],
                        preferred_element_type=F32)
            outs.append(o * (1.0 / l))
        qoff = pl.multiple_of(rl * GRID_W, GRID_W)
        o_ref[pl.ds(qoff, GRID_W), :] = jnp.concatenate(outs, axis=1).astype(BF16)
        return carry

    _flash_pair(NA_ROWS_PER_STEP, qk, sm_pv, 0)


def _na_attn(q, k, v, bias, B, S):
    rows = S // GRID_W
    assert rows % NA_ROWS_PER_STEP == 0 and rows >= NA_KEY_ROWS
    N = B * S
    tq = NA_ROWS_PER_STEP * GRID_W
    blk = NA_KEY_BLOCK_ROWS * GRID_W
    bpb = rows // NA_KEY_BLOCK_ROWS

    def kv_spec(i):
        def idx(b, rb):
            kw0 = jnp.clip(rb * NA_ROWS_PER_STEP - NA_WIN_R // 2, 0, rows - NA_KEY_ROWS)
            return (b * bpb + kw0 // NA_KEY_BLOCK_ROWS + i, 0)
        return pl.BlockSpec((blk, NA_W), idx)

    kv_specs = [kv_spec(i) for i in range(NA_KEY_ROWS // NA_KEY_BLOCK_ROWS)]
    qspec = pl.BlockSpec((tq, NA_W), lambda b, rb: (b * (rows // NA_ROWS_PER_STEP) + rb, 0))
    return pl.pallas_call(
        functools.partial(_na_kernel, rows=rows),
        grid=(B, rows // NA_ROWS_PER_STEP),
        in_specs=[qspec] + kv_specs + kv_specs + [_const_spec(bias.shape)],
        out_specs=qspec,
        out_shape=jax.ShapeDtypeStruct((N, NA_W), BF16),
        scratch_shapes=[pltpu.VMEM((NA_HEADS, NA_KEY_ROWS * GRID_W, HEAD_DIM), BF16),
                        pltpu.VMEM((NA_HEADS, NA_KEY_ROWS * GRID_W, HEAD_DIM), BF16),
                        pltpu.VMEM((2, NA_HEADS, GRID_W, NA_WIN_R * GRID_W), F32)],
        compiler_params=_params(2),
        name="na_attn",
    )(q, k, k, k, k, v, v, v, v, bias)


def _na_bias_table(rpb):
    c = np.arange(GRID_W)[:, None]
    cp = np.arange(GRID_W)[None, :]
    win0 = np.clip(c - NA_WIN_C // 2, 0, GRID_W - NA_WIN_C)
    ok = (cp >= win0) & (cp < win0 + NA_WIN_C)
    cidx = np.clip(cp - c + NA_WIN_C - 1, 0, 2 * NA_WIN_C - 2)
    t = rpb.astype(F32)[:, :, cidx]
    t = jnp.where(jnp.asarray(ok)[None, None], t, -jnp.inf)
    per = [t[:, d:d + NA_WIN_R].transpose(0, 2, 1, 3).reshape(NA_HEADS, GRID_W, NA_WIN_R * GRID_W)
           for d in range(NA_WIN_R)]
    return jnp.stack(per, axis=0)


def _flash_pair(n_chunks, qk, sm_pv, init):
    def body(i, carry):
        c = 2 * i
        qk(1, c + 1)
        carry = sm_pv(0, c, carry)
        qk(0, c + 2)
        return sm_pv(1, c + 1, carry)

    qk(0, 0)
    carry = lax.fori_loop(0, n_chunks // 2 - 1, body, init)
    qk(1, n_chunks - 1)
    carry = sm_pv(0, n_chunks - 2, carry)
    return sm_pv(1, n_chunks - 1, carry)


def _online_softmax_step(s, m_prev, a_prev, vt_chunk, off=None):
    m_chunk = jnp.max(s, axis=0, keepdims=True)
    if off is not None:
        m_chunk = m_chunk - off
    m_new = jnp.maximum(m_prev, m_chunk)
    alpha = jnp.exp2(m_prev - m_new)
    p = jnp.exp2(s - (m_new if off is None else m_new + off)).astype(BF16)
    a_new = alpha * a_prev + jnp.dot(vt_chunk, p, preferred_element_type=F32)
    return m_new, a_new


def _diff_kernel(slopes_ref, qt_ref, k_ref, vt_ref, lamv_ref, gsub_ref, o_ref, s_scr, b_scr,
                 *, S, tq, tkc, hp, lam_init):
    g = pl.program_id(1)
    qi = pl.program_id(2)
    row = lax.broadcasted_iota(jnp.int32, (tkc, tq), 0)
    col = lax.broadcasted_iota(jnp.int32, (tkc, tq), 1)
    u = (col - row).astype(F32)
    lv = lamv_ref[...]
    lam = (jnp.exp(jnp.sum(lv[0:1] * lv[1:2], axis=-1, keepdims=True))
           - jnp.exp(jnp.sum(lv[2:3] * lv[3:4], axis=-1, keepdims=True)) + lam_init)
    zeros = jnp.zeros((DIFF_QK_DIM, tq), BF16)

    for hh in range(hp):
        slope = slopes_ref[g * hp + hh]
        ksl = slice(hh * 128, (hh + 1) * 128)
        qpair = qt_ref[ksl, :]
        qts = (jnp.concatenate([qpair[:DIFF_QK_DIM], zeros], axis=0),
               jnp.concatenate([zeros, qpair[DIFF_QK_DIM:]], axis=0))

        def qk(slot, c):
            k0 = pl.multiple_of(c * tkc, tkc)
            kc = k_ref[pl.ds(k0, tkc), ksl]
            for j in range(2):
                s_scr[slot, j] = jnp.dot(kc, qts[j], preferred_element_type=F32)

        b_scr[0] = slope * u
        b_scr[1] = -slope * u
        b_scr[2] = slope * jnp.abs(u)

        def sm_pv(slot, c, carry):
            k0 = pl.multiple_of(c * tkc, tkc)
            dc = qi - jnp.asarray(c, jnp.int32)
            side = jnp.where(dc > 0, 0, jnp.where(dc < 0, 1, 2))
            off = slope * (jnp.abs(dc) * tq).astype(F32)
            vc = vt_ref[hh * DIFF_VT:(hh + 1) * DIFF_VT, pl.ds(k0, tkc)]
            out = []
            for j in range(2):
                m_new, a_new = _online_softmax_step(
                    s_scr[slot, j] - b_scr[side], carry[2 * j], carry[2 * j + 1], vc, off)
                out += [m_new, a_new]
            return tuple(out)

        init = (jnp.full((1, tq), -jnp.inf, F32), jnp.zeros((DIFF_VT, tq), F32)) * 2
        _, a0, _, a1 = _flash_pair(S // tkc, qk, sm_pv, init)
        o0 = a0[:DIFF_V_DIM] * (1.0 / a0[DIFF_V_DIM:DIFF_V_DIM + 1])
        o1 = a1[:DIFF_V_DIM] * (1.0 / a1[DIFF_V_DIM:DIFF_V_DIM + 1])
        o = o0 - lam * o1
        ms = jnp.mean(o * o, axis=0, keepdims=True)
        y = (o * lax.rsqrt(ms + EPS)) * gsub_ref[...]
        o_ref[ksl, :] = (y * (1.0 - lam_init)).astype(BF16)


def _diff_attn(qt, k, vt, lamv, gsub, B, S, lam_init):
    tq = tkc = ATT_TQ
    hp = 2 if S <= 4096 else 1
    assert S % tq == 0 and (S // tkc) % 2 == 0 and DIFF_HEADS % hp == 0
    nq = S // tq
    slopes = jnp.asarray(
        2.0 ** (-8.0 * np.arange(1, DIFF_HEADS + 1) / DIFF_HEADS) * LOG2E, F32)
    return pl.pallas_call(
        functools.partial(_diff_kernel, S=S, tq=tq, tkc=tkc, hp=hp, lam_init=lam_init),
        grid=(B, DIFF_HEADS // hp, nq),
        in_specs=[pl.BlockSpec(memory_space=pltpu.SMEM),
                  pl.BlockSpec((None, hp * 128, tq), lambda b, g, i: (b, g, i)),
                  pl.BlockSpec((S, hp * 128), lambda b, g, i: (b, g)),
                  pl.BlockSpec((None, hp * DIFF_VT, S), lambda b, g, i: (b, g, 0)),
                  _const_spec((4, DIFF_QK_DIM)),
                  _const_spec((DIFF_V_DIM, 1))],
        out_specs=pl.BlockSpec((None, hp * 128, tq), lambda b, g, i: (b, g, i)),
        out_shape=jax.ShapeDtypeStruct((B, DIFF_W, S), BF16),
        scratch_shapes=[pltpu.VMEM((2, 2, tkc, tq), F32), pltpu.VMEM((3, tkc, tq), F32)],
        compiler_params=_params(3),
        name="diff_attn",
    )(slopes, qt, k, vt, lamv, gsub)


def _ffn_kernel(*refs, n_tok):
    a_refs = refs[:n_tok + 1]
    (x_ref, wo_ref, gpm_ref, gpf_ref, wg_ref, wu_ref, wout_ref, gpost_ref,
     y_ref) = refs[n_tok + 1:]
    m = None
    r0 = 0
    for a_ref in a_refs[:n_tok]:
        w = a_ref.shape[1]
        t = jnp.dot(a_ref[...], wo_ref[r0:r0 + w, :], preferred_element_type=F32)
        m = t if m is None else m + t
        r0 += w
    at_ref = a_refs[n_tok]
    t = lax.dot_general(at_ref[...], wo_ref[r0:r0 + at_ref.shape[0], :], _TN,
                        preferred_element_type=F32)
    m = t if m is None else m + t
    x1 = x_ref[...] + _rms(m, gpm_ref[...])
    h = _rms(x1, gpf_ref[...]).astype(BF16)
    acc = jnp.zeros(x1.shape, F32)
    for c in range(D_FF // FF_CHUNK):
        sl = slice(c * FF_CHUNK, (c + 1) * FF_CHUNK)
        g = jnp.dot(h, wg_ref[:, sl], preferred_element_type=F32)
        u = jnp.dot(h, wu_ref[:, sl], preferred_element_type=F32)
        act = (g * (1.0 / (1.0 + jnp.exp(-g)))) * u
        acc = acc + jnp.dot(act.astype(BF16), wout_ref[sl, :], preferred_element_type=F32)
    y_ref[...] = x1 + _rms(acc, gpost_ref[...])


def _out_ffn(a_tok, a_t, x, wo, gpm, gpf, wg, wu, wout, gpost, S):
    N = x.shape[0]
    tn = TOKEN_TILE
    spb = S // tn
    wt = a_t.shape[1]
    return pl.pallas_call(
        functools.partial(_ffn_kernel, n_tok=len(a_tok)),
        grid=(N // tn,),
        in_specs=[pl.BlockSpec((tn, a.shape[1]), lambda i: (i, 0)) for a in a_tok]
        + [pl.BlockSpec((None, wt, tn), lambda i: (i // spb, 0, i % spb)),
           pl.BlockSpec((tn, D_MODEL), lambda i: (i, 0)),
           _const_spec((D_MODEL, D_MODEL)),
           _const_spec((1, D_MODEL)), _const_spec((1, D_MODEL)),
           _const_spec((D_MODEL, D_FF)), _const_spec((D_MODEL, D_FF)),
           _const_spec((D_FF, D_MODEL)), _const_spec((1, D_MODEL))],
        out_specs=pl.BlockSpec((tn, D_MODEL), lambda i: (i, 0)),
        out_shape=jax.ShapeDtypeStruct((N, D_MODEL), F32),
        compiler_params=_params(1),
        name="out_ffn",
    )(*a_tok, a_t, x, wo, gpm, gpf, wg, wu, wout, gpost)


MLA_Q_GROUP = 4


def _mla_proj_kernel(x_ref, g_ref, wd_ref, gq_ref, gkv_ref, wqt_ref, wqrt_ref,
                     wkn_ref, wvt_ref, ctt_ref, stt_ref, ck_ref, sk_ref,
                     qt_ref, k_ref, vt_ref):
    h = _rms(x_ref[...], g_ref[...]).astype(BF16)
    c = jnp.dot(h, wd_ref[...], preferred_element_type=F32)
    cqn = _rms(c[:, :MLA_Q_RANK], gq_ref[...]).astype(BF16)
    o1 = MLA_Q_RANK + MLA_KV_RANK
    ckvn = _rms(c[:, MLA_Q_RANK:o1], gkv_ref[...]).astype(BF16)
    krope = c[:, o1:o1 + MLA_QK_PAD] * ck_ref[...] + c[:, o1 + MLA_QK_PAD:] * sk_ref[...]

    gw = MLA_Q_GROUP * MLA_QK_PAD
    ctt = jnp.concatenate([ctt_ref[...]] * MLA_Q_GROUP, axis=0)
    stt = jnp.concatenate([stt_ref[...]] * MLA_Q_GROUP, axis=0)
    for p in range(MLA_HEADS // MLA_Q_GROUP):
        sl = slice(p * gw, (p + 1) * gw)
        a = lax.dot_general(wqt_ref[sl, :], cqn, _NT, preferred_element_type=F32)
        b = lax.dot_general(wqrt_ref[sl, :], cqn, _NT, preferred_element_type=F32)
        qt_ref[sl, :] = (a * ctt + b * stt).astype(BF16)

    for p in range(MLA_HEADS // MLA_Q_GROUP):
        sl = slice(p * gw, (p + 1) * gw)
        kn = jnp.dot(ckvn, wkn_ref[:, sl], preferred_element_type=F32)
        k_ref[:, sl] = (kn + jnp.concatenate([krope] * MLA_Q_GROUP, axis=1)).astype(BF16)

    vt = lax.dot_general(wvt_ref[...], ckvn, _NT, preferred_element_type=F32)
    vt_ref[...] = _with_ones_rows(vt, MLA_VT, MLA_V).astype(BF16)


def _mla_proj(x, p1, tables, B, S):
    N = B * S
    tn = TOKEN_TILE
    spb = S // tn
    qw = MLA_HEADS * MLA_QK_PAD
    nvt = MLA_HEADS * MLA_VT
    ctt, stt, ck, sk = tables
    return pl.pallas_call(
        _mla_proj_kernel,
        grid=(N // tn,),
        in_specs=[pl.BlockSpec((tn, D_MODEL), lambda i: (i, 0)),
                  _const_spec((1, D_MODEL)),
                  _const_spec(p1['wd'].shape), _const_spec((1, MLA_Q_RANK)),
                  _const_spec((1, MLA_KV_RANK)),
                  _const_spec(p1['wqt'].shape), _const_spec(p1['wqrt'].shape),
                  _const_spec(p1['wkn'].shape), _const_spec(p1['wvt'].shape),
                  pl.BlockSpec((MLA_QK_PAD, tn), lambda i: (0, i % spb)),
                  pl.BlockSpec((MLA_QK_PAD, tn), lambda i: (0, i % spb)),
                  pl.BlockSpec((tn, MLA_QK_PAD), lambda i: (i % spb, 0)),
                  pl.BlockSpec((tn, MLA_QK_PAD), lambda i: (i % spb, 0))],
        out_specs=[pl.BlockSpec((None, qw, tn), lambda i: (i // spb, 0, i % spb)),
                   pl.BlockSpec((tn, qw), lambda i: (i, 0)),
                   pl.BlockSpec((None, nvt, tn), lambda i: (i // spb, 0, i % spb))],
        out_shape=[jax.ShapeDtypeStruct((B, qw, S), BF16),
                   jax.ShapeDtypeStruct((N, qw), BF16),
                   jax.ShapeDtypeStruct((B, nvt, S), BF16)],
        compiler_params=_params(1),
        name="mla_proj",
    )(x, p1['g'], p1['wd'], p1['gq'], p1['gkv'], p1['wqt'], p1['wqrt'],
      p1['wkn'], p1['wvt'], ctt, stt, ck, sk)


def _mla_kernel(qt_ref, k_ref, vt_ref, o_ref, s_scr, *, S, tq, tkc, hp):
    for h0 in range(0, hp, 2):
        heads = (h0, h0 + 1)
        qts = [qt_ref[h * MLA_QK_PAD:(h + 1) * MLA_QK_PAD, :] for h in heads]

        def qk(slot, c):
            k0 = pl.multiple_of(c * tkc, tkc)
            for j, h in enumerate(heads):
                kc = k_ref[pl.ds(k0, tkc), h * MLA_QK_PAD:(h + 1) * MLA_QK_PAD]
                s_scr[slot, j] = jnp.dot(kc, qts[j], preferred_element_type=F32)

        def sm_pv(slot, c, carry):
            k0 = pl.multiple_of(c * tkc, tkc)
            out = []
            for j, h in enumerate(heads):
                vc = vt_ref[h * MLA_VT:(h + 1) * MLA_VT, pl.ds(k0, tkc)]
                m_new, a_new = _online_softmax_step(
                    s_scr[slot, j], carry[2 * j], carry[2 * j + 1], vc)
                out += [m_new, a_new]
            return tuple(out)

        init = (jnp.full((1, tq), -jnp.inf, F32), jnp.zeros((MLA_VT, tq), F32)) * 2
        res = _flash_pair(S // tkc, qk, sm_pv, init)
        for j, h in enumerate(heads):
            a = res[2 * j + 1]
            o_ref[h * MLA_V:(h + 1) * MLA_V, :] = (
                a[:MLA_V] * (1.0 / a[MLA_V:MLA_V + 1])).astype(BF16)


def _mla_attn(qt, k, vt, B, S):
    tq, tkc = ATT_TQ, ATT_TKC
    hp = 4 if S <= 4096 else 2
    assert S % tq == 0 and (S // tkc) % 2 == 0 and MLA_HEADS % hp == 0
    nq = S // tq
    return pl.pallas_call(
        functools.partial(_mla_kernel, S=S, tq=tq, tkc=tkc, hp=hp),
        grid=(B, MLA_HEADS // hp, nq),
        in_specs=[pl.BlockSpec((None, hp * MLA_QK_PAD, tq), lambda b, g, i: (b, g, i)),
                  pl.BlockSpec((S, hp * MLA_QK_PAD), lambda b, g, i: (b, g)),
                  pl.BlockSpec((None, hp * MLA_VT, S), lambda b, g, i: (b, g, 0))],
        out_specs=pl.BlockSpec((None, hp * MLA_V, tq), lambda b, g, i: (b, g, i)),
        out_shape=jax.ShapeDtypeStruct((B, MLA_HEADS * MLA_V, S), BF16),
        scratch_shapes=[pltpu.VMEM((2, 2, tkc, tq), F32)],
        compiler_params=_params(3),
        name="mla_attn",
    )(qt, k, vt)


def _rot_half_cols(w):
    half = MLA_ROPE // 2
    return jnp.concatenate([-w[..., half:], w[..., :half]], axis=-1)


def _prep_layer0(p):
    w = p['w_in']
    qscale = HEAD_DIM ** -0.5
    o0 = 3 * NA_W
    w_main = jnp.concatenate([w[:, 0:NA_W] * qscale, w[:, NA_W:2 * NA_W], w[:, 2 * NA_W:o0],
                              w[:, o0 + DIFF_W:o0 + 2 * DIFF_W]], axis=1).astype(BF16)
    wq_t = (w[:, o0:o0 + DIFF_W] * (DIFF_QK_DIM ** -0.5)).T
    wv3 = w[:, o0 + 2 * DIFF_W:].T.reshape(DIFF_HEADS, DIFF_V_DIM, D_MODEL)
    wv_t = jnp.concatenate([wv3, jnp.zeros((DIFF_HEADS, ONES_PAD, D_MODEL), F32)],
                           axis=1).reshape(DIFF_HEADS * DIFF_VT, D_MODEL)
    wt = jnp.concatenate([wq_t, wv_t], axis=0).astype(BF16)
    lamv = jnp.stack([p['lam_q1'], p['lam_k1'], p['lam_q2'], p['lam_k2']]).astype(F32)
    return dict(w_main=w_main, wt=wt, lamv=lamv,
                bias=_na_bias_table(p['rpb']),
                gsub=p['g_sub'].reshape(-1, 1).astype(F32))


def _prep_ffn(p):
    w = p['w_ffn_in']
    return dict(wo=p['w_out'].astype(BF16),
                gpm=p['g_post_mix'].reshape(1, -1), gpf=p['g_pre_ffn'].reshape(1, -1),
                wg=w[:, :D_FF].astype(BF16), wu=w[:, D_FF:].astype(BF16),
                wout=p['w_ffn_out'].astype(BF16), gpost=p['g_post_ffn'].reshape(1, -1))


def _prep_layer1(p):
    wd = p['w_down']
    o1 = MLA_Q_RANK + MLA_KV_RANK
    kr = wd[:, o1:]
    zl = jnp.zeros((D_MODEL, MLA_NOPE), F32)
    zr = jnp.zeros((D_MODEL, MLA_QK_PAD - MLA_QK), F32)
    wd_all = jnp.concatenate([wd[:, :o1], zl, kr, zr, zl, _rot_half_cols(kr), zr],
                             axis=1).astype(BF16)
    wq3 = p['w_uq'].reshape(MLA_Q_RANK, MLA_HEADS, MLA_QK)
    zq = jnp.zeros((MLA_Q_RANK, MLA_HEADS, MLA_QK_PAD - MLA_QK), F32)
    wq = jnp.concatenate([wq3, zq], axis=-1).reshape(MLA_Q_RANK, -1)
    zn = jnp.zeros((MLA_Q_RANK, MLA_HEADS, MLA_NOPE), F32)
    wqr = jnp.concatenate([zn, _rot_half_cols(wq3[..., MLA_NOPE:]), zq],
                          axis=-1).reshape(MLA_Q_RANK, -1)
    wkv3 = p['w_ukv'].reshape(MLA_KV_RANK, MLA_HEADS, MLA_NOPE + MLA_V)
    zk = jnp.zeros((MLA_KV_RANK, MLA_HEADS, MLA_QK_PAD - MLA_NOPE), F32)
    wkn = jnp.concatenate([wkv3[..., :MLA_NOPE], zk], axis=-1).reshape(MLA_KV_RANK, -1)
    wv3 = wkv3[..., MLA_NOPE:].transpose(1, 2, 0)
    wvt = jnp.concatenate([wv3, jnp.zeros((MLA_HEADS, ONES_PAD, MLA_KV_RANK), F32)],
                          axis=1).reshape(MLA_HEADS * MLA_VT, MLA_KV_RANK)
    return dict(wd=wd_all, wqt=wq.T.astype(BF16), wqrt=wqr.T.astype(BF16),
                wkn=wkn.astype(BF16), wvt=wvt.astype(BF16),
                g=p['g_pre_mix'].reshape(1, -1), gq=p['g_q'].reshape(1, -1),
                gkv=p['g_kv'].reshape(1, -1))


def _rope_tables(S):
    half = MLA_ROPE // 2
    inv = jnp.asarray(ROPE_THETA ** (-np.arange(half, dtype=np.float32) * 2.0 / MLA_ROPE), F32)
    ang = jnp.arange(S).astype(F32)[:, None] * inv[None, :]
    cos2 = jnp.concatenate([jnp.cos(ang)] * 2, axis=1)
    sin2 = jnp.concatenate([jnp.sin(ang)] * 2, axis=1)
    ones = jnp.ones((S, MLA_NOPE), F32)
    z_nope = jnp.zeros((S, MLA_NOPE), F32)
    z_pad = jnp.zeros((S, MLA_QK_PAD - MLA_QK), F32)
    ck = jnp.concatenate([z_nope, cos2, z_pad], axis=1)
    sk = jnp.concatenate([z_nope, sin2, z_pad], axis=1)
    qscale = MLA_QK ** -0.5 * LOG2E
    ctt = (jnp.concatenate([ones, cos2, z_pad], axis=1) * qscale).T
    stt = (sk * qscale).T
    return ctt, stt, ck, sk


def _trunk(x3, p0, f0, p1, f1, g0):
    B, S, _ = x3.shape
    x = x3.reshape(B * S, D_MODEL)
    lam_init0 = 0.8 - 0.6 * math.exp(-0.3 * 0)
    naq, nak, nav, dk, dqt, dvt = _proj0(x, g0, p0['w_main'], p0['wt'], B, S)
    a = _na_attn(naq, nak, nav, p0['bias'], B, S)
    bt = _diff_attn(dqt, dk, dvt, p0['lamv'], p0['gsub'], B, S, lam_init0)
    x = _out_ffn([a], bt, x, f0['wo'], f0['gpm'], f0['gpf'], f0['wg'], f0['wu'],
                 f0['wout'], f0['gpost'], S)
    qt, k, vt = _mla_proj(x, p1, _rope_tables(S), B, S)
    ot = _mla_attn(qt, k, vt, B, S)
    x = _out_ffn([], ot, x, f1['wo'], f1['gpm'], f1['gpf'], f1['wg'], f1['wu'],
                 f1['wout'], f1['gpost'], S)
    return x.reshape(B, S, D_MODEL)


def kernel(x_prompt, x_sample, l0_g_pre_mix, l0_w_in, l0_rpb, l0_lam_q1, l0_lam_k1, l0_lam_q2, l0_lam_k2, l0_g_sub, l0_w_out, l0_g_post_mix, l0_g_pre_ffn, l0_w_ffn_in, l0_w_ffn_out, l0_g_post_ffn, l1_g_pre_mix, l1_w_down, l1_g_q, l1_g_kv, l1_w_uq, l1_w_ukv, l1_w_out, l1_g_post_mix, l1_g_pre_ffn, l1_w_ffn_in, l1_w_ffn_out, l1_g_post_ffn):
    p0 = _prep_layer0(dict(w_in=l0_w_in, rpb=l0_rpb, lam_q1=l0_lam_q1, lam_k1=l0_lam_k1,
                           lam_q2=l0_lam_q2, lam_k2=l0_lam_k2, g_sub=l0_g_sub))
    f0 = _prep_ffn(dict(w_out=l0_w_out, g_post_mix=l0_g_post_mix, g_pre_ffn=l0_g_pre_ffn,
                        w_ffn_in=l0_w_ffn_in, w_ffn_out=l0_w_ffn_out, g_post_ffn=l0_g_post_ffn))
    p1 = _prep_layer1(dict(g_pre_mix=l1_g_pre_mix, w_down=l1_w_down, g_q=l1_g_q, g_kv=l1_g_kv,
                           w_uq=l1_w_uq, w_ukv=l1_w_ukv))
    f1 = _prep_ffn(dict(w_out=l1_w_out, g_post_mix=l1_g_post_mix, g_pre_ffn=l1_g_pre_ffn,
                        w_ffn_in=l1_w_ffn_in, w_ffn_out=l1_w_ffn_out, g_post_ffn=l1_g_post_ffn))
    g0 = l0_g_pre_mix.reshape(1, -1)
    y_prompt = _trunk(x_prompt, p0, f0, p1, f1, g0)
    y_sample = _trunk(x_sample, p0, f0, p1, f1, g0)
    return (y_prompt, y_sample)
```

```python
import functools
import math

import numpy as np
import jax
import jax.numpy as jnp
from jax import lax
from jax.experimental import pallas as pl
from jax.experimental.pallas import tpu as pltpu

F32 = jnp.float32
BF16 = jnp.bfloat16

D_MODEL = 1024
GRID_W = 64
HEAD_DIM = 64
EPS = 1e-6

NA_HEADS = 8
NA_WIN_R = 8
NA_WIN_C = 16
NA_W = NA_HEADS * HEAD_DIM
NA_ROWS_PER_STEP = 8
NA_KEY_ROWS = 16
NA_KEY_BLOCK_ROWS = 4

DIFF_HEADS = 4
DIFF_QK_DIM = 64
DIFF_V_DIM = 128
DIFF_W = 512

MLA_HEADS = 16
MLA_Q_RANK = 384
MLA_KV_RANK = 256
MLA_NOPE = 64
MLA_ROPE = 32
MLA_V = 64
MLA_QK = MLA_NOPE + MLA_ROPE
MLA_QK_PAD = 128
ROPE_THETA = 10000.0

D_FF = 2816
FF_CHUNK = 256

VMEM_LIMIT = 56 * 1024 * 1024

TOKEN_TILE = 512
ATT_TQ = 512
ATT_TKC = 512
ONES_PAD = 16
DIFF_VT = DIFF_V_DIM + ONES_PAD
MLA_VT = MLA_V + ONES_PAD
LOG2E = math.log2(math.e)

_NT = (((1,), (1,)), ((), ()))
_TN = (((0,), (0,)), ((), ()))


def _rms(x, g):
    ms = jnp.mean(x * x, axis=-1, keepdims=True)
    return (x * lax.rsqrt(ms + EPS)) * g


def _const_spec(shape):
    nd = len(shape)
    return pl.BlockSpec(shape, lambda *_: (0,) * nd, pipeline_mode=pl.Buffered(1))


def _params(n_axes):
    return pltpu.CompilerParams(
        dimension_semantics=("parallel",) * n_axes, vmem_limit_bytes=VMEM_LIMIT)


def _with_ones_rows(vt, per_head, data_rows):
    row = lax.broadcasted_iota(jnp.int32, vt.shape, 0)
    return jnp.where(row % per_head == data_rows, 1.0, vt)


def _proj0_kernel(x_ref, g_ref, w_ref, wt_ref,
                  naq_ref, nak_ref, nav_ref, dk_ref, dqt_ref, dvt_ref):
    h = _rms(x_ref[...], g_ref[...]).astype(BF16)
    for i, o_ref in enumerate((naq_ref, nak_ref, nav_ref, dk_ref)):
        o_ref[...] = jnp.dot(h, w_ref[:, i * 512:(i + 1) * 512],
                             preferred_element_type=F32).astype(BF16)
    t = lax.dot_general(wt_ref[...], h, _NT, preferred_element_type=F32)
    dqt_ref[...] = (t[:DIFF_W] * LOG2E).astype(BF16)
    dvt_ref[...] = _with_ones_rows(t[DIFF_W:], DIFF_VT, DIFF_V_DIM).astype(BF16)


def _proj0(x, g, w, wt, B, S):
    N = B * S
    tn = TOKEN_TILE
    spb = S // tn
    tok = pl.BlockSpec((tn, 512), lambda i: (i, 0))
    nvt = DIFF_HEADS * DIFF_VT
    return pl.pallas_call(
        _proj0_kernel,
        grid=(N // tn,),
        in_specs=[pl.BlockSpec((tn, D_MODEL), lambda i: (i, 0)),
                  _const_spec((1, D_MODEL)),
                  _const_spec(w.shape),
                  _const_spec(wt.shape)],
        out_specs=[tok, tok, tok, tok,
                   pl.BlockSpec((None, DIFF_W, tn), lambda i: (i // spb, 0, i % spb)),
                   pl.BlockSpec((None, nvt, tn), lambda i: (i // spb, 0, i % spb))],
        out_shape=[jax.ShapeDtypeStruct((N, 512), BF16)] * 4
        + [jax.ShapeDtypeStruct((B, DIFF_W, S), BF16),
           jax.ShapeDtypeStruct((B, nvt, S), BF16)],
        compiler_params=_params(1),
        name="proj0",
    )(x, g, w, wt)


def _na_kernel(q_ref, k0, k1, k2, k3, v0, v1, v2, v3, bias_ref, o_ref,
               kbuf, vbuf, s_scr, *, rows):
    rb = pl.program_id(1)
    blk = NA_KEY_BLOCK_ROWS * GRID_W
    for i, (kr, vr) in enumerate(((k0, v0), (k1, v1), (k2, v2), (k3, v3))):
        kblk = kr[...]
        vblk = vr[...]
        for h in range(NA_HEADS):
            sl = slice(h * HEAD_DIM, (h + 1) * HEAD_DIM)
            kbuf[h, i * blk:(i + 1) * blk, :] = kblk[:, sl]
            vbuf[h, i * blk:(i + 1) * blk, :] = vblk[:, sl]
    kw0 = jnp.clip(rb * NA_ROWS_PER_STEP - NA_WIN_R // 2, 0, rows - NA_KEY_ROWS)
    nkeys = NA_WIN_R * GRID_W

    def window(rl):
        r = rb * NA_ROWS_PER_STEP + rl
        rs = jnp.clip(r - NA_WIN_R // 2, 0, rows - NA_WIN_R)
        off = pl.multiple_of((rs - kw0) * GRID_W, GRID_W)
        return off, rs - r + NA_WIN_R - 1

    def qk(slot, rl):
        off, dr0 = window(rl)
        qrow = q_ref[pl.ds(pl.multiple_of(rl * GRID_W, GRID_W), GRID_W), :]
        for h in range(NA_HEADS):
            sl = slice(h * HEAD_DIM, (h + 1) * HEAD_DIM)
            s = lax.dot_general(qrow[:, sl], kbuf[h, pl.ds(off, nkeys), :], _NT,
                                preferred_element_type=F32)
            s_scr[slot, h] = s + bias_ref[dr0, h]

    def sm_pv(slot, rl, carry):
        off, _ = window(rl)
        outs = []
        for h in range(NA_HEADS):
            s = s_scr[slot, h]
            m = jnp.max(s, axis=-1, keepdims=True)
            p = jnp.exp(s - m)
            l = jnp.sum(p, axis=-1, keepdims=True)
            o = jnp.dot(p.astype(BF16), vbuf[h, pl.ds(off, nkeys), :],
                        preferred_element_type=F32)
            outs.append(o * (1.0 / l))
        qoff = pl.multiple_of(rl * GRID_W, GRID_W)
        o_ref[pl.ds(qoff, GRID_W), :] = jnp.concatenate(outs, axis=1).astype(BF16)
        return carry

    _flash_pair(NA_ROWS_PER_STEP, qk, sm_pv, 0)


def _na_attn(q, k, v, bias, B, S):
    rows = S // GRID_W
    assert rows % NA_ROWS_PER_STEP == 0 and rows >= NA_KEY_ROWS
    N = B * S
    tq = NA_ROWS_PER_STEP * GRID_W
    blk = NA_KEY_BLOCK_ROWS * GRID_W
    bpb = rows // NA_KEY_BLOCK_ROWS

    def kv_spec(i):
        def idx(b, rb):
            kw0 = jnp.clip(rb * NA_ROWS_PER_STEP - NA_WIN_R // 2, 0, rows - NA_KEY_ROWS)
            return (b * bpb + kw0 // NA_KEY_BLOCK_ROWS + i, 0)
        return pl.BlockSpec((blk, NA_W), idx)

    kv_specs = [kv_spec(i) for i in range(NA_KEY_ROWS // NA_KEY_BLOCK_ROWS)]
    qspec = pl.BlockSpec((tq, NA_W), lambda b, rb: (b * (rows // NA_ROWS_PER_STEP) + rb, 0))
    return pl.pallas_call(
        functools.partial(_na_kernel, rows=rows),
        grid=(B, rows // NA_ROWS_PER_STEP),
        in_specs=[qspec] + kv_specs + kv_specs + [_const_spec(bias.shape)],
        out_specs=qspec,
        out_shape=jax.ShapeDtypeStruct((N, NA_W), BF16),
        scratch_shapes=[pltpu.VMEM((NA_HEADS, NA_KEY_ROWS * GRID_W, HEAD_DIM), BF16),
                        pltpu.VMEM((NA_HEADS, NA_KEY_ROWS * GRID_W, HEAD_DIM), BF16),
                        pltpu.VMEM((2, NA_HEADS, GRID_W, NA_WIN_R * GRID_W), F32)],
        compiler_params=_params(2),
        name="na_attn",
    )(q, k, k, k, k, v, v, v, v, bias)


def _na_bias_table(rpb):
    c = np.arange(GRID_W)[:, None]
    cp = np.arange(GRID_W)[None, :]
    win0 = np.clip(c - NA_WIN_C // 2, 0, GRID_W - NA_WIN_C)
    ok = (cp >= win0) & (cp < win0 + NA_WIN_C)
    cidx = np.clip(cp - c + NA_WIN_C - 1, 0, 2 * NA_WIN_C - 2)
    t = rpb.astype(F32)[:, :, cidx]
    t = jnp.where(jnp.asarray(ok)[None, None], t, -jnp.inf)
    per = [t[:, d:d + NA_WIN_R].transpose(0, 2, 1, 3).reshape(NA_HEADS, GRID_W, NA_WIN_R * GRID_W)
           for d in range(NA_WIN_R)]
    return jnp.stack(per, axis=0)


def _flash_pair(n_chunks, qk, sm_pv, init):
    def body(i, carry):
        c = 2 * i
        qk(1, c + 1)
        carry = sm_pv(0, c, carry)
        qk(0, c + 2)
        return sm_pv(1, c + 1, carry)

    qk(0, 0)
    carry = lax.fori_loop(0, n_chunks // 2 - 1, body, init)
    qk(1, n_chunks - 1)
    carry = sm_pv(0, n_chunks - 2, carry)
    return sm_pv(1, n_chunks - 1, carry)


def _online_softmax_step(s, m_prev, a_prev, vt_chunk, off=None):
    m_chunk = jnp.max(s, axis=0, keepdims=True)
    if off is not None:
        m_chunk = m_chunk - off
    m_new = jnp.maximum(m_prev, m_chunk)
    alpha = jnp.exp2(m_prev - m_new)
    p = jnp.exp2(s - (m_new if off is None else m_new + off)).astype(BF16)
    a_new = alpha * a_prev + jnp.dot(vt_chunk, p, preferred_element_type=F32)
    return m_new, a_new


def _diff_kernel(slopes_ref, qt_ref, k_ref, vt_ref, lamv_ref, gsub_ref, o_ref, s_scr, b_scr,
                 *, S, tq, tkc, hp, lam_init):
    g = pl.program_id(1)
    qi = pl.program_id(2)
    row = lax.broadcasted_iota(jnp.int32, (tkc, tq), 0)
    col = lax.broadcasted_iota(jnp.int32, (tkc, tq), 1)
    u = (col - row).astype(F32)
    lv = lamv_ref[...]
    lam = (jnp.exp(jnp.sum(lv[0:1] * lv[1:2], axis=-1, keepdims=True))
           - jnp.exp(jnp.sum(lv[2:3] * lv[3:4], axis=-1, keepdims=True)) + lam_init)
    zeros = jnp.zeros((DIFF_QK_DIM, tq), BF16)

    for hh in range(hp):
        slope = slopes_ref[g * hp + hh]
        ksl = slice(hh * 128, (hh + 1) * 128)
        qpair = qt_ref[ksl, :]
        qts = (jnp.concatenate([qpair[:DIFF_QK_DIM], zeros], axis=0),
               jnp.concatenate([zeros, qpair[DIFF_QK_DIM:]], axis=0))

        def qk(slot, c):
            k0 = pl.multiple_of(c * tkc, tkc)
            kc = k_ref[pl.ds(k0, tkc), ksl]
            for j in range(2):
                s_scr[slot, j] = jnp.dot(kc, qts[j], preferred_element_type=F32)

        b_scr[0] = slope * u
        b_scr[1] = -slope * u
        b_scr[2] = slope * jnp.abs(u)

        def sm_pv(slot, c, carry):
            k0 = pl.multiple_of(c * tkc, tkc)
            dc = qi - jnp.asarray(c, jnp.int32)
            side = jnp.where(dc > 0, 0, jnp.where(dc < 0, 1, 2))
            off = slope * (jnp.abs(dc) * tq).astype(F32)
            vc = vt_ref[hh * DIFF_VT:(hh + 1) * DIFF_VT, pl.ds(k0, tkc)]
            out = []
            for j in range(2):
                m_new, a_new = _online_softmax_step(
                    s_scr[slot, j] - b_scr[side], carry[2 * j], carry[2 * j + 1], vc, off)
                out += [m_new, a_new]
            return tuple(out)

        init = (jnp.full((1, tq), -jnp.inf, F32), jnp.zeros((DIFF_VT, tq), F32)) * 2
        _, a0, _, a1 = _flash_pair(S // tkc, qk, sm_pv, init)
        o0 = a0[:DIFF_V_DIM] * (1.0 / a0[DIFF_V_DIM:DIFF_V_DIM + 1])
        o1 = a1[:DIFF_V_DIM] * (1.0 / a1[DIFF_V_DIM:DIFF_V_DIM + 1])
        o = o0 - lam * o1
        ms = jnp.mean(o * o, axis=0, keepdims=True)
        y = (o * lax.rsqrt(ms + EPS)) * gsub_ref[...]
        o_ref[ksl, :] = (y * (1.0 - lam_init)).astype(BF16)


def _diff_attn(qt, k, vt, lamv, gsub, B, S, lam_init):
    tq = tkc = ATT_TQ
    hp = 2 if S <= 4096 else 1
    assert S % tq == 0 and (S // tkc) % 2 == 0 and DIFF_HEADS % hp == 0
    nq = S // tq
    slopes = jnp.asarray(
        2.0 ** (-8.0 * np.arange(1, DIFF_HEADS + 1) / DIFF_HEADS) * LOG2E, F32)
    return pl.pallas_call(
        functools.partial(_diff_kernel, S=S, tq=tq, tkc=tkc, hp=hp, lam_init=lam_init),
        grid=(B, DIFF_HEADS // hp, nq),
        in_specs=[pl.BlockSpec(memory_space=pltpu.SMEM),
                  pl.BlockSpec((None, hp * 128, tq), lambda b, g, i: (b, g, i)),
                  pl.BlockSpec((S, hp * 128), lambda b, g, i: (b, g)),
                  pl.BlockSpec((None, hp * DIFF_VT, S), lambda b, g, i: (b, g, 0)),
                  _const_spec((4, DIFF_QK_DIM)),
                  _const_spec((DIFF_V_DIM, 1))],
        out_specs=pl.BlockSpec((None, hp * 128, tq), lambda b, g, i: (b, g, i)),
        out_shape=jax.ShapeDtypeStruct((B, DIFF_W, S), BF16),
        scratch_shapes=[pltpu.VMEM((2, 2, tkc, tq), F32), pltpu.VMEM((3, tkc, tq), F32)],
        compiler_params=_params(3),
        name="diff_attn",
    )(slopes, qt, k, vt, lamv, gsub)


def _ffn_kernel(*refs, n_tok):
    a_refs = refs[:n_tok + 1]
    (x_ref, wo_ref, gpm_ref, gpf_ref, wg_ref, wu_ref, wout_ref, gpost_ref,
     y_ref) = refs[n_tok + 1:]
    m = None
    r0 = 0
    for a_ref in a_refs[:n_tok]:
        w = a_ref.shape[1]
        t = jnp.dot(a_ref[...], wo_ref[r0:r0 + w, :], preferred_element_type=F32)
        m = t if m is None else m + t
        r0 += w
    at_ref = a_refs[n_tok]
    t = lax.dot_general(at_ref[...], wo_ref[r0:r0 + at_ref.shape[0], :], _TN,
                        preferred_element_type=F32)
    m = t if m is None else m + t
    x1 = x_ref[...] + _rms(m, gpm_ref[...])
    h = _rms(x1, gpf_ref[...]).astype(BF16)
    acc = jnp.zeros(x1.shape, F32)
    for c in range(D_FF // FF_CHUNK):
        sl = slice(c * FF_CHUNK, (c + 1) * FF_CHUNK)
        g = jnp.dot(h, wg_ref[:, sl], preferred_element_type=F32)
        u = jnp.dot(h, wu_ref[:, sl], preferred_element_type=F32)
        act = (g * (1.0 / (1.0 + jnp.exp(-g)))) * u
        acc = acc + jnp.dot(act.astype(BF16), wout_ref[sl, :], preferred_element_type=F32)
    y_ref[...] = x1 + _rms(acc, gpost_ref[...])


def _out_ffn(a_tok, a_t, x, wo, gpm, gpf, wg, wu, wout, gpost, S):
    N = x.shape[0]
    tn = TOKEN_TILE
    spb = S // tn
    wt = a_t.shape[1]
    return pl.pallas_call(
        functools.partial(_ffn_kernel, n_tok=len(a_tok)),
        grid=(N // tn,),
        in_specs=[pl.BlockSpec((tn, a.shape[1]), lambda i: (i, 0)) for a in a_tok]
        + [pl.BlockSpec((None, wt, tn), lambda i: (i // spb, 0, i % spb)),
           pl.BlockSpec((tn, D_MODEL), lambda i: (i, 0)),
           _const_spec((D_MODEL, D_MODEL)),
           _const_spec((1, D_MODEL)), _const_spec((1, D_MODEL)),
           _const_spec((D_MODEL, D_FF)), _const_spec((D_MODEL, D_FF)),
           _const_spec((D_FF, D_MODEL)), _const_spec((1, D_MODEL))],
        out_specs=pl.BlockSpec((tn, D_MODEL), lambda i: (i, 0)),
        out_shape=jax.ShapeDtypeStruct((N, D_MODEL), F32),
        compiler_params=_params(1),
        name="out_ffn",
    )(*a_tok, a_t, x, wo, gpm, gpf, wg, wu, wout, gpost)


MLA_Q_GROUP = 4


def _mla_proj_kernel(x_ref, g_ref, wd_ref, gq_ref, gkv_ref, wqt_ref, wqrt_ref,
                     wkn_ref, wvt_ref, ctt_ref, stt_ref, ck_ref, sk_ref,
                     qt_ref, k_ref, vt_ref):
    h = _rms(x_ref[...], g_ref[...]).astype(BF16)
    c = jnp.dot(h, wd_ref[...], preferred_element_type=F32)
    cqn = _rms(c[:, :MLA_Q_RANK], gq_ref[...]).astype(BF16)
    o1 = MLA_Q_RANK + MLA_KV_RANK
    ckvn = _rms(c[:, MLA_Q_RANK:o1], gkv_ref[...]).astype(BF16)
    krope = c[:, o1:o1 + MLA_QK_PAD] * ck_ref[...] + c[:, o1 + MLA_QK_PAD:] * sk_ref[...]

    gw = MLA_Q_GROUP * MLA_QK_PAD
    rw = MLA_Q_GROUP * MLA_ROPE
    tn = cqn.shape[0]
    z_nope = jnp.zeros((MLA_NOPE, tn), F32)
    z_pad = jnp.zeros((MLA_QK_PAD - MLA_QK, tn), F32)
    ctt = jnp.concatenate([ctt_ref[...]] * MLA_Q_GROUP, axis=0)
    stt = jnp.concatenate([stt_ref[...]] * MLA_Q_GROUP, axis=0)
    for p in range(MLA_HEADS // MLA_Q_GROUP):
        sl = slice(p * gw, (p + 1) * gw)
        a = lax.dot_general(wqt_ref[sl, :], cqn, _NT, preferred_element_type=F32)
        bc = lax.dot_general(wqrt_ref[p * rw:(p + 1) * rw, :], cqn, _NT,
                             preferred_element_type=F32)
        b = jnp.concatenate(
            [piece for hh in range(MLA_Q_GROUP)
             for piece in (z_nope, bc[hh * MLA_ROPE:(hh + 1) * MLA_ROPE], z_pad)], axis=0)
        qt_ref[sl, :] = (a * ctt + b * stt).astype(BF16)

    for p in range(MLA_HEADS // MLA_Q_GROUP):
        sl = slice(p * gw, (p + 1) * gw)
        kn = jnp.dot(ckvn, wkn_ref[:, sl], preferred_element_type=F32)
        k_ref[:, sl] = (kn + jnp.concatenate([krope] * MLA_Q_GROUP, axis=1)).astype(BF16)

    vt = lax.dot_general(wvt_ref[...], ckvn, _NT, preferred_element_type=F32)
    vt_ref[...] = _with_ones_rows(vt, MLA_VT, MLA_V).astype(BF16)


def _mla_proj(x, p1, tables, B, S):
    N = B * S
    tn = TOKEN_TILE
    spb = S // tn
    qw = MLA_HEADS * MLA_QK_PAD
    nvt = MLA_HEADS * MLA_VT
    ctt, stt, ck, sk = tables
    return pl.pallas_call(
        _mla_proj_kernel,
        grid=(N // tn,),
        in_specs=[pl.BlockSpec((tn, D_MODEL), lambda i: (i, 0)),
                  _const_spec((1, D_MODEL)),
                  _const_spec(p1['wd'].shape), _const_spec((1, MLA_Q_RANK)),
                  _const_spec((1, MLA_KV_RANK)),
                  _const_spec(p1['wqt'].shape), _const_spec(p1['wqrt'].shape),
                  _const_spec(p1['wkn'].shape), _const_spec(p1['wvt'].shape),
                  pl.BlockSpec((MLA_QK_PAD, tn), lambda i: (0, i % spb)),
                  pl.BlockSpec((MLA_QK_PAD, tn), lambda i: (0, i % spb)),
                  pl.BlockSpec((tn, MLA_QK_PAD), lambda i: (i % spb, 0)),
                  pl.BlockSpec((tn, MLA_QK_PAD), lambda i: (i % spb, 0))],
        out_specs=[pl.BlockSpec((None, qw, tn), lambda i: (i // spb, 0, i % spb)),
                   pl.BlockSpec((tn, qw), lambda i: (i, 0)),
                   pl.BlockSpec((None, nvt, tn), lambda i: (i // spb, 0, i % spb))],
        out_shape=[jax.ShapeDtypeStruct((B, qw, S), BF16),
                   jax.ShapeDtypeStruct((N, qw), BF16),
                   jax.ShapeDtypeStruct((B, nvt, S), BF16)],
        compiler_params=_params(1),
        name="mla_proj",
    )(x, p1['g'], p1['wd'], p1['gq'], p1['gkv'], p1['wqt'], p1['wqrt'],
      p1['wkn'], p1['wvt'], ctt, stt, ck, sk)


def _mla_kernel(qt_ref, k_ref, vt_ref, o_ref, s_scr, *, S, tq, tkc, hp):
    for h0 in range(0, hp, 2):
        heads = (h0, h0 + 1)
        qts = [qt_ref[h * MLA_QK_PAD:(h + 1) * MLA_QK_PAD, :] for h in heads]

        def qk(slot, c):
            k0 = pl.multiple_of(c * tkc, tkc)
            for j, h in enumerate(heads):
                kc = k_ref[pl.ds(k0, tkc), h * MLA_QK_PAD:(h + 1) * MLA_QK_PAD]
                s_scr[slot, j] = jnp.dot(kc, qts[j], preferred_element_type=F32)

        def sm_pv(slot, c, carry):
            k0 = pl.multiple_of(c * tkc, tkc)
            out = []
            for j, h in enumerate(heads):
                vc = vt_ref[h * MLA_VT:(h + 1) * MLA_VT, pl.ds(k0, tkc)]
                m_new, a_new = _online_softmax_step(
                    s_scr[slot, j], carry[2 * j], carry[2 * j + 1], vc)
                out += [m_new, a_new]
            return tuple(out)

        init = (jnp.full((1, tq), -jnp.inf, F32), jnp.zeros((MLA_VT, tq), F32)) * 2
        res = _flash_pair(S // tkc, qk, sm_pv, init)
        for j, h in enumerate(heads):
            a = res[2 * j + 1]
            o_ref[h * MLA_V:(h + 1) * MLA_V, :] = (
                a[:MLA_V] * (1.0 / a[MLA_V:MLA_V + 1])).astype(BF16)


def _mla_attn(qt, k, vt, B, S):
    tq, tkc = ATT_TQ, ATT_TKC
    hp = 4 if S <= 4096 else 2
    assert S % tq == 0 and (S // tkc) % 2 == 0 and MLA_HEADS % hp == 0
    nq = S // tq
    return pl.pallas_call(
        functools.partial(_mla_kernel, S=S, tq=tq, tkc=tkc, hp=hp),
        grid=(B, MLA_HEADS // hp, nq),
        in_specs=[pl.BlockSpec((None, hp * MLA_QK_PAD, tq), lambda b, g, i: (b, g, i)),
                  pl.BlockSpec((S, hp * MLA_QK_PAD), lambda b, g, i: (b, g)),
                  pl.BlockSpec((None, hp * MLA_VT, S), lambda b, g, i: (b, g, 0))],
        out_specs=pl.BlockSpec((None, hp * MLA_V, tq), lambda b, g, i: (b, g, i)),
        out_shape=jax.ShapeDtypeStruct((B, MLA_HEADS * MLA_V, S), BF16),
        scratch_shapes=[pltpu.VMEM((2, 2, tkc, tq), F32)],
        compiler_params=_params(3),
        name="mla_attn",
    )(qt, k, vt)


def _rot_half_cols(w):
    half = MLA_ROPE // 2
    return jnp.concatenate([-w[..., half:], w[..., :half]], axis=-1)


def _prep_layer0(p):
    w = p['w_in']
    qscale = HEAD_DIM ** -0.5
    o0 = 3 * NA_W
    w_main = jnp.concatenate([w[:, 0:NA_W] * qscale, w[:, NA_W:2 * NA_W], w[:, 2 * NA_W:o0],
                              w[:, o0 + DIFF_W:o0 + 2 * DIFF_W]], axis=1).astype(BF16)
    wq_t = (w[:, o0:o0 + DIFF_W] * (DIFF_QK_DIM ** -0.5)).T
    wv3 = w[:, o0 + 2 * DIFF_W:].T.reshape(DIFF_HEADS, DIFF_V_DIM, D_MODEL)
    wv_t = jnp.concatenate([wv3, jnp.zeros((DIFF_HEADS, ONES_PAD, D_MODEL), F32)],
                           axis=1).reshape(DIFF_HEADS * DIFF_VT, D_MODEL)
    wt = jnp.concatenate([wq_t, wv_t], axis=0).astype(BF16)
    lamv = jnp.stack([p['lam_q1'], p['lam_k1'], p['lam_q2'], p['lam_k2']]).astype(F32)
    return dict(w_main=w_main, wt=wt, lamv=lamv,
                bias=_na_bias_table(p['rpb']),
                gsub=p['g_sub'].reshape(-1, 1).astype(F32))


def _prep_ffn(p):
    w = p['w_ffn_in']
    return dict(wo=p['w_out'].astype(BF16),
                gpm=p['g_post_mix'].reshape(1, -1), gpf=p['g_pre_ffn'].reshape(1, -1),
                wg=w[:, :D_FF].astype(BF16), wu=w[:, D_FF:].astype(BF16),
                wout=p['w_ffn_out'].astype(BF16), gpost=p['g_post_ffn'].reshape(1, -1))


def _prep_layer1(p):
    wd = p['w_down']
    o1 = MLA_Q_RANK + MLA_KV_RANK
    kr = wd[:, o1:]
    zl = jnp.zeros((D_MODEL, MLA_NOPE), F32)
    zr = jnp.zeros((D_MODEL, MLA_QK_PAD - MLA_QK), F32)
    wd_all = jnp.concatenate([wd[:, :o1], zl, kr, zr, zl, _rot_half_cols(kr), zr],
                             axis=1).astype(BF16)
    wq3 = p['w_uq'].reshape(MLA_Q_RANK, MLA_HEADS, MLA_QK)
    zq = jnp.zeros((MLA_Q_RANK, MLA_HEADS, MLA_QK_PAD - MLA_QK), F32)
    wq = jnp.concatenate([wq3, zq], axis=-1).reshape(MLA_Q_RANK, -1)
    wqr = _rot_half_cols(wq3[..., MLA_NOPE:]).reshape(MLA_Q_RANK, -1)
    wkv3 = p['w_ukv'].reshape(MLA_KV_RANK, MLA_HEADS, MLA_NOPE + MLA_V)
    zk = jnp.zeros((MLA_KV_RANK, MLA_HEADS, MLA_QK_PAD - MLA_NOPE), F32)
    wkn = jnp.concatenate([wkv3[..., :MLA_NOPE], zk], axis=-1).reshape(MLA_KV_RANK, -1)
    wv3 = wkv3[..., MLA_NOPE:].transpose(1, 2, 0)
    wvt = jnp.concatenate([wv3, jnp.zeros((MLA_HEADS, ONES_PAD, MLA_KV_RANK), F32)],
                          axis=1).reshape(MLA_HEADS * MLA_VT, MLA_KV_RANK)
    return dict(wd=wd_all, wqt=wq.T.astype(BF16), wqrt=wqr.T.astype(BF16),
                wkn=wkn.astype(BF16), wvt=wvt.astype(BF16),
                g=p['g_pre_mix'].reshape(1, -1), gq=p['g_q'].reshape(1, -1),
                gkv=p['g_kv'].reshape(1, -1))


def _rope_tables(S):
    half = MLA_ROPE // 2
    inv = jnp.asarray(ROPE_THETA ** (-np.arange(half, dtype=np.float32) * 2.0 / MLA_ROPE), F32)
    ang = jnp.arange(S).astype(F32)[:, None] * inv[None, :]
    cos2 = jnp.concatenate([jnp.cos(ang)] * 2, axis=1)
    sin2 = jnp.concatenate([jnp.sin(ang)] * 2, axis=1)
    ones = jnp.ones((S, MLA_NOPE), F32)
    z_nope = jnp.zeros((S, MLA_NOPE), F32)
    z_pad = jnp.zeros((S, MLA_QK_PAD - MLA_QK), F32)
    ck = jnp.concatenate([z_nope, cos2, z_pad], axis=1)
    sk = jnp.concatenate([z_nope, sin2, z_pad], axis=1)
    qscale = MLA_QK ** -0.5 * LOG2E
    ctt = (jnp.concatenate([ones, cos2, z_pad], axis=1) * qscale).T
    stt = (sk * qscale).T
    return ctt, stt, ck, sk


def _trunk(x3, p0, f0, p1, f1, g0):
    B, S, _ = x3.shape
    x = x3.reshape(B * S, D_MODEL)
    lam_init0 = 0.8 - 0.6 * math.exp(-0.3 * 0)
    naq, nak, nav, dk, dqt, dvt = _proj0(x, g0, p0['w_main'], p0['wt'], B, S)
    a = _na_attn(naq, nak, nav, p0['bias'], B, S)
    bt = _diff_attn(dqt, dk, dvt, p0['lamv'], p0['gsub'], B, S, lam_init0)
    x = _out_ffn([a], bt, x, f0['wo'], f0['gpm'], f0['gpf'], f0['wg'], f0['wu'],
                 f0['wout'], f0['gpost'], S)
    qt, k, vt = _mla_proj(x, p1, _rope_tables(S), B, S)
    ot = _mla_attn(qt, k, vt, B, S)
    x = _out_ffn([], ot, x, f1['wo'], f1['gpm'], f1['gpf'], f1['wg'], f1['wu'],
                 f1['wout'], f1['gpost'], S)
    return x.reshape(B, S, D_MODEL)


def kernel(x_prompt, x_sample, l0_g_pre_mix, l0_w_in, l0_rpb, l0_lam_q1, l0_lam_k1, l0_lam_q2, l0_lam_k2, l0_g_sub, l0_w_out, l0_g_post_mix, l0_g_pre_ffn, l0_w_ffn_in, l0_w_ffn_out, l0_g_post_ffn, l1_g_pre_mix, l1_w_down, l1_g_q, l1_g_kv, l1_w_uq, l1_w_ukv, l1_w_out, l1_g_post_mix, l1_g_pre_ffn, l1_w_ffn_in, l1_w_ffn_out, l1_g_post_ffn):
    p0 = _prep_layer0(dict(w_in=l0_w_in, rpb=l0_rpb, lam_q1=l0_lam_q1, lam_k1=l0_lam_k1,
                           lam_q2=l0_lam_q2, lam_k2=l0_lam_k2, g_sub=l0_g_sub))
    f0 = _prep_ffn(dict(w_out=l0_w_out, g_post_mix=l0_g_post_mix, g_pre_ffn=l0_g_pre_ffn,
                        w_ffn_in=l0_w_ffn_in, w_ffn_out=l0_w_ffn_out, g_post_ffn=l0_g_post_ffn))
    p1 = _prep_layer1(dict(g_pre_mix=l1_g_pre_mix, w_down=l1_w_down, g_q=l1_g_q, g_kv=l1_g_kv,
                           w_uq=l1_w_uq, w_ukv=l1_w_ukv))
    f1 = _prep_ffn(dict(w_out=l1_w_out, g_post_mix=l1_g_post_mix, g_pre_ffn=l1_g_pre_ffn,
                        w_ffn_in=l1_w_ffn_in, w_ffn_out=l1_w_ffn_out, g_post_ffn=l1_g_post_ffn))
    g0 = l0_g_pre_mix.reshape(1, -1)
    y_prompt = _trunk(x_prompt, p0, f0, p1, f1, g0)
    y_sample = _trunk(x_sample, p0, f0, p1, f1, g0)
    return (y_prompt, y_sample)
```
